```python
import math
import jax, jax.numpy as jnp
from jax import lax
import numpy as np

D_MODEL = 4096
BATCH = 2
SEQ = 8192
DEPTH = 2

CTX_LEN = 256
GRID_W = 64
N_MIXERS = 2
EPS = 1e-6
DA_HEADS = 16
DA_HEAD_DIM = 128
ROPE_AXIS_DIM = DA_HEAD_DIM // 2
ROPE_BASE = 10000.0
Q_BLOCK = 128
ML_HEADS = 8
ML_QK_DIM = D_MODEL // 2 // ML_HEADS
ML_V_DIM = D_MODEL // ML_HEADS
ML_QK_W = ML_HEADS * ML_QK_DIM
ML_V_W = ML_HEADS * ML_V_DIM
ML_IN_W = 2 * ML_QK_W + ML_V_W + D_MODEL + 4 * ML_HEADS
CHUNK = 64
GATE_CAP = 15.0
N_EXPERTS = 16
EXPERT_FF = D_MODEL // 4
CAPACITY_FACTOR = 2
N_ATTN_LAYERS = (DEPTH + 1) // 2
N_MLSTM_LAYERS = DEPTH // 2

kernel_name = "hybrid_diffattn_mlstm_ecmoe_dit"


def rmsnorm(x, g):
    xf = x.astype(jnp.float32)
    y = xf * lax.rsqrt(jnp.mean(xf * xf, axis=-1, keepdims=True) + EPS)
    return (y * g.astype(jnp.float32)).astype(x.dtype)


def axial_rope(n_rows):
    row = jnp.repeat(jnp.arange(n_rows), GRID_W).astype(jnp.float32)
    col = jnp.tile(jnp.arange(GRID_W), n_rows).astype(jnp.float32)
    inv = ROPE_BASE ** (-jnp.arange(0, ROPE_AXIS_DIM, 2, dtype=jnp.float32) / ROPE_AXIS_DIM)
    ang_r = (row[:, None] * inv)[:, None, None, :]
    ang_c = (col[:, None] * inv)[:, None, None, :]
    return (jnp.cos(ang_r), jnp.sin(ang_r), jnp.cos(ang_c), jnp.sin(ang_c))


def rotate(x, cos, sin):
    half = x.shape[-1] // 2
    x1, x2 = x[..., :half], x[..., half:]
    return jnp.concatenate([x1 * cos - x2 * sin, x1 * sin + x2 * cos], axis=-1)


def apply_axial_rope(x, rope):
    cr, sr, cc, sc = rope
    xf = x.astype(jnp.float32)
    out = jnp.concatenate([rotate(xf[..., :ROPE_AXIS_DIM], cr, sr),
                           rotate(xf[..., ROPE_AXIS_DIM:], cc, sc)], axis=-1)
    return out.astype(x.dtype)


def diff_attention(h_lat, h_ctx, w_in, w_out, lam_vecs, subln_g, lam_init, rope, ctx_out):
    B, T, _ = h_lat.shape

    def project(h):
        q, k, v = jnp.split(h @ w_in, 3, axis=-1)
        n = h.shape[1]
        return (q.reshape(B, n, DA_HEADS, 2, DA_HEAD_DIM), k.reshape(B, n, DA_HEADS, 2, DA_HEAD_DIM),
                v.reshape(B, n, DA_HEADS, 2 * DA_HEAD_DIM))

    ql, kl, vl = project(h_lat)
    qc, kc, vc = project(h_ctx)
    ql = apply_axial_rope(ql, rope)
    kl = apply_axial_rope(kl, rope)
    lv = lam_vecs.astype(jnp.float32)
    lam = jnp.exp(jnp.sum(lv[0] * lv[1])) - jnp.exp(jnp.sum(lv[2] * lv[3])) + lam_init
    scale = DA_HEAD_DIM ** -0.5

    def attend(qb, keys, vals):
        s = jnp.einsum('bqhpd,bkhpd->bhpqk', qb, keys).astype(jnp.float32) * scale
        p = jax.nn.softmax(s, axis=-1)
        a = p[:, :, 0] - lam * p[:, :, 1]
        return jnp.einsum('bhqk,bkhv->bqhv', a.astype(vals.dtype), vals)

    k_all = jnp.concatenate([kc, kl], axis=1)
    v_all = jnp.concatenate([vc, vl], axis=1)
    nb = T // Q_BLOCK
    q_blocks = jnp.moveaxis(ql.reshape(B, nb, Q_BLOCK, DA_HEADS, 2, DA_HEAD_DIM), 1, 0)
    o_lat = lax.map(lambda qb: attend(qb, k_all, v_all), q_blocks)
    o_lat = jnp.moveaxis(o_lat, 0, 1).reshape(B, T, DA_HEADS, 2 * DA_HEAD_DIM)

    def finish(o):
        o = rmsnorm(o, subln_g) * (1.0 - lam_init)
        return o.reshape(B, o.shape[1], D_MODEL) @ w_out

    y_lat = finish(o_lat)
    y_ctx = finish(attend(qc, kc, vc)) if ctx_out else None
    return y_lat, y_ctx


def mlstm_chunkwise(q, k, v, i_pre, f_pre, state):
    B, T, H, _ = q.shape
    nc = T // CHUNK

    def to_chunks(a):
        return jnp.moveaxis(a.reshape((B, nc, CHUNK, H) + a.shape[3:]), (1, 3), (0, 2))

    qc, kc, vc, ic = to_chunks(q), to_chunks(k), to_chunks(v), to_chunks(i_pre)
    lfc = jax.nn.log_sigmoid(to_chunks(f_pre))
    tri = jnp.tril(jnp.ones((CHUNK, CHUNK), dtype=bool))

    def step(carry, inp):
        C, n, m = carry
        qb, kb, vb, ib, lfb = inp
        b = jnp.cumsum(lfb, axis=-1)
        a = b + m[..., None]
        d = jnp.where(tri, b[..., :, None] - b[..., None, :] + ib[..., None, :], -jnp.inf)
        m_t = jnp.maximum(a, jnp.max(d, axis=-1))
        w = jnp.exp(d - m_t[..., None]) * jnp.einsum('bhld,bhsd->bhls', qb, kb)
        inter = jnp.exp(a - m_t)
        num = inter[..., None] * jnp.einsum('bhvk,bhlk->bhlv', C, qb) + jnp.einsum('bhls,bhsv->bhlv', w, vb)
        den = inter * jnp.einsum('bhk,bhlk->bhl', n, qb) + jnp.sum(w, axis=-1)
        h = num / jnp.maximum(jnp.abs(den), jnp.exp(-m_t))[..., None]
        b_last = b[..., -1]
        g = b_last[..., None] - b + ib
        m_new = jnp.maximum(b_last + m, jnp.max(g, axis=-1))
        decay = jnp.exp(b_last + m - m_new)
        wk = jnp.exp(g - m_new[..., None])[..., None] * kb
        C_new = decay[..., None, None] * C + jnp.einsum('bhlv,bhlk->bhvk', vb, wk)
        n_new = decay[..., None] * n + jnp.sum(wk, axis=-2)
        return (C_new, n_new, m_new), h

    state, h = lax.scan(step, state, (qc, kc, vc, ic, lfc))
    h = jnp.moveaxis(h, (0, 2), (1, 3)).reshape(B, T, H, v.shape[-1])
    return h, state


def mlstm_mixer(h_lat, h_ctx, w_in, b_gates, norm_g, w_out, ctx_out):
    B = h_lat.shape[0]

    def project(h):
        n = h.shape[1]
        q, k, v, o, gates = jnp.split(h @ w_in, [ML_QK_W, 2 * ML_QK_W, 2 * ML_QK_W + ML_V_W,
                                              2 * ML_QK_W + ML_V_W + D_MODEL], axis=-1)
        q = q.reshape(B, n, ML_HEADS, ML_QK_DIM).astype(jnp.float32) * (ML_QK_DIM ** -0.5)
        k = k.reshape(B, n, ML_HEADS, ML_QK_DIM).astype(jnp.float32)
        v = v.reshape(B, n, ML_HEADS, ML_V_DIM).astype(jnp.float32)
        gates = gates.reshape(B, n, 4, ML_HEADS).astype(jnp.float32) + b_gates.astype(jnp.float32)
        gates = GATE_CAP * jnp.tanh(gates / GATE_CAP)
        return q, k, v, o, gates

    qc, kc, vc, oc, gc = project(h_ctx)
    ql, kl, vl, ol, gl = project(h_lat)
    state0 = (jnp.zeros((B, ML_HEADS, ML_V_DIM, ML_QK_DIM), jnp.float32),
              jnp.zeros((B, ML_HEADS, ML_QK_DIM), jnp.float32),
              jnp.zeros((B, ML_HEADS), jnp.float32))

    def run(direction, flip):
        gi, gf = 2 * direction, 2 * direction + 1
        prep = (lambda a: a[:, ::-1]) if flip else (lambda a: a)
        hc, st = mlstm_chunkwise(prep(qc), prep(kc), prep(vc), prep(gc[:, :, gi]), prep(gc[:, :, gf]), state0)
        hl, _ = mlstm_chunkwise(prep(ql), prep(kl), prep(vl), prep(gl[:, :, gi]), prep(gl[:, :, gf]), st)
        return prep(hc), prep(hl)

    hc_f, hl_f = run(0, False)
    hc_b, hl_b = run(1, True)

    def finish(hsum, o):
        n = hsum.shape[1]
        hn = rmsnorm(hsum, norm_g.reshape(ML_HEADS, ML_V_DIM)).astype(h_lat.dtype).reshape(B, n, D_MODEL)
        return (hn * jax.nn.sigmoid(o)) @ w_out

    y_lat = finish(hl_f + hl_b, ol)
    y_ctx = finish(hc_f + hc_b, oc) if ctx_out else None
    return y_lat, y_ctx


def expert_choice_moe(h, router_w, w_gate, w_up, w_down):
    n_tok = h.shape[1]
    cap = max(1, (CAPACITY_FACTOR * n_tok) // N_EXPERTS)

    def per_set(hs):
        aff = jax.nn.softmax((hs @ router_w).astype(jnp.float32), axis=-1)
        g, idx = lax.top_k(aff.T, cap)
        xe = hs[idx]
        hid = jax.nn.silu(jnp.einsum('ecd,edf->ecf', xe, w_gate)) * jnp.einsum('ecd,edf->ecf', xe, w_up)
        ye = jnp.einsum('ecf,efd->ecd', hid, w_down) * g[..., None].astype(hs.dtype)
        return jnp.zeros_like(hs).at[idx.reshape(-1)].add(ye.reshape(-1, hs.shape[-1]))

    return jax.vmap(per_set)(h)


def setup_inputs(seed: int = 0) -> dict:
    key = jax.random.key(seed)
    ks = jax.random.split(key, 20)
    D = D_MODEL
    f32 = jnp.float32

    def nrm(k, shape, s):
        return jax.random.normal(k, shape, f32) * s

    gate_base = jnp.stack([jnp.full((ML_HEADS,), -3.0, f32), jnp.linspace(3.0, 6.0, ML_HEADS, dtype=f32)] * 2)
    return {
        "x": nrm(ks[0], (BATCH, SEQ, D), 1.0),
        "c": nrm(ks[1], (BATCH, D), 1.0),
        "ctx": nrm(ks[2], (BATCH, CTX_LEN, D), 1.0),
        "c_ctx": nrm(ks[3], (D,), 1.0),
        "ada_w": nrm(ks[4], (DEPTH, D, 6 * D), 0.5 * D ** -0.5),
        "ada_b": nrm(ks[5], (DEPTH, 6 * D), 0.02),
        "norm_g": 1.0 + nrm(ks[6], (DEPTH, 4, D), 0.05),
        "attn_w_in": nrm(ks[7], (N_ATTN_LAYERS, D, 3 * D), D ** -0.5),
        "attn_w_out": nrm(ks[8], (N_ATTN_LAYERS, D, D), D ** -0.5),
        "attn_lambda": nrm(ks[9], (N_ATTN_LAYERS, 4, DA_HEAD_DIM), 0.1),
        "attn_subln_g": 1.0 + nrm(ks[10], (N_ATTN_LAYERS, 2 * DA_HEAD_DIM), 0.05),
        "mlstm_w_in": nrm(ks[11], (N_MLSTM_LAYERS, D, ML_IN_W), D ** -0.5),
        "mlstm_b_gates": gate_base[None] + nrm(ks[12], (N_MLSTM_LAYERS, 4, ML_HEADS), 0.1),
        "mlstm_norm_g": 1.0 + nrm(ks[13], (N_MLSTM_LAYERS, D), 0.05),
        "mlstm_w_out": nrm(ks[14], (N_MLSTM_LAYERS, D, D), D ** -0.5),
        "router_w": nrm(ks[15], (DEPTH, D, N_EXPERTS), D ** -0.5),
        "expert_w_gate": nrm(ks[16], (DEPTH, N_EXPERTS, D, EXPERT_FF), D ** -0.5),
        "expert_w_up": nrm(ks[17], (DEPTH, N_EXPERTS, D, EXPERT_FF), D ** -0.5),
        "expert_w_down": nrm(ks[18], (DEPTH, N_EXPERTS, EXPERT_FF, D), EXPERT_FF ** -0.5),
    }


def reference(x, c, ctx, c_ctx, ada_w, ada_b, norm_g, attn_w_in, attn_w_out, attn_lambda, attn_subln_g,
              mlstm_w_in, mlstm_b_gates, mlstm_norm_g, mlstm_w_out, router_w, expert_w_gate, expert_w_up,
              expert_w_down):
    n_rows = x.shape[1] // GRID_W
    rope = axial_rope(n_rows)
    silu_c = jax.nn.silu(c)
    silu_cc = jax.nn.silu(c_ctx)
    xl, xc = x, ctx
    for i in range(DEPTH):
        last = i == DEPTH - 1
        sh1, sc1, g1, sh2, sc2, g2 = jnp.split((silu_c @ ada_w[i] + ada_b[i])[:, None, :], 6, axis=-1)
        csh1, csc1, cg1, csh2, csc2, cg2 = jnp.split(silu_cc @ ada_w[i] + ada_b[i], 6, axis=-1)
        g_pre_mix, g_post_mix, g_pre_ffn, g_post_ffn = norm_g[i, 0], norm_g[i, 1], norm_g[i, 2], norm_g[i, 3]

        hl = rmsnorm(xl, g_pre_mix) * (1.0 + sc1) + sh1
        hc = rmsnorm(xc, g_pre_mix) * (1.0 + csc1) + csh1
        j = i // N_MIXERS
        if i % N_MIXERS == 0:
            lam_init = 0.8 - 0.6 * math.exp(-0.3 * i)
            yl, yc = diff_attention(hl, hc, attn_w_in[j], attn_w_out[j], attn_lambda[j], attn_subln_g[j],
                                    lam_init, rope, not last)
        else:
            yl, yc = mlstm_mixer(hl, hc, mlstm_w_in[j], mlstm_b_gates[j], mlstm_norm_g[j], mlstm_w_out[j],
                                 not last)
        xl = xl + g1 * rmsnorm(yl, g_post_mix)
        hl = rmsnorm(xl, g_pre_ffn) * (1.0 + sc2) + sh2
        xl = xl + g2 * rmsnorm(expert_choice_moe(hl, router_w[i], expert_w_gate[i], expert_w_up[i],
                                                 expert_w_down[i]), g_post_ffn)
        if not last:
            xc = xc + cg1 * rmsnorm(yc, g_post_mix)
            hc = rmsnorm(xc, g_pre_ffn) * (1.0 + csc2) + csh2
            xc = xc + cg2 * rmsnorm(expert_choice_moe(hc, router_w[i], expert_w_gate[i], expert_w_up[i],
                                                     expert_w_down[i]), g_post_ffn)
    return xl
```

```python
import functools
import math
from typing import NamedTuple

import jax
import jax.numpy as jnp
from jax import lax
from jax.experimental import pallas as pl
from jax.experimental.pallas import tpu as pltpu

BF16 = jnp.bfloat16
F32 = jnp.float32

EPS = 1e-6
GATE_CAP = 15.0
ROPE_BASE = 10000.0
LANES = 128
MIB = 1024 * 1024
VMEM_LIMIT = 56 * MIB


class Cfg(NamedTuple):
    D: int
    B: int
    T: int
    Tc: int
    grid_w: int
    da_heads: int
    da_dim: int
    ml_heads: int
    E: int
    F: int
    chunk: int
    cap_factor: int

    @property
    def n_lat(self):
        return self.B * self.T

    @property
    def n_ctx(self):
        return self.B * self.Tc

    @property
    def R(self):
        return self.n_lat + self.n_ctx

    @property
    def ml_qk(self):
        return self.D // 2 // self.ml_heads

    @property
    def ml_v(self):
        return self.D // self.ml_heads


FULL_CFG = Cfg(D=4096, B=2, T=8192, Tc=256, grid_w=64, da_heads=16, da_dim=128, ml_heads=8, E=16, F=1024,
               chunk=64, cap_factor=2)


def _cparams(sem, vmem=VMEM_LIMIT):
    return pltpu.CompilerParams(dimension_semantics=sem, vmem_limit_bytes=vmem)


def _tile(n, pref):
    t = min(n, pref)
    while n % t:
        t //= 2
    return t


def _lane_tile(x, reps):
    return x if reps == 1 else jnp.concatenate([x] * reps, axis=1)


def _adaln_kernel(c_ref, w_ref, b_ref, o_ref):
    c = c_ref[...]
    s = (c * jax.nn.sigmoid(c)).astype(BF16)
    o_ref[...] = jnp.dot(s, w_ref[...].astype(BF16), preferred_element_type=F32) + b_ref[...]


def adaln(cvec, ada_w, ada_b3, layer):
    D = cvec.shape[1]
    N = ada_w.shape[2]
    tn = _tile(N, 512)
    return pl.pallas_call(
        _adaln_kernel,
        grid=(N // tn,),
        in_specs=[
            pl.BlockSpec((8, D), lambda j: (0, 0)),
            pl.BlockSpec((None, D, tn), lambda j: (layer, 0, j)),
            pl.BlockSpec((None, 1, tn), lambda j: (layer, 0, j)),
        ],
        out_specs=pl.BlockSpec((8, tn), lambda j: (0, j)),
        out_shape=jax.ShapeDtypeStruct((8, N), F32),
        compiler_params=_cparams(("parallel",)),
        name="adaln",
    )(cvec, ada_w, ada_b3)


def _rms(x, g):
    return x * lax.rsqrt(jnp.mean(x * x, axis=-1, keepdims=True) + EPS) * g


def _resid_norm_kernel(*refs, has_y, want_h, want_logits):
    it = iter(refs)
    x_ref = next(it)
    if has_y:
        y_ref, gate_ref, gpost_ref = next(it), next(it), next(it)
    if want_h:
        gpre_ref, sc_ref, sh_ref = next(it), next(it), next(it)
    if want_logits:
        rwh_ref, rwl_ref = next(it), next(it)
    if has_y:
        xo_ref = next(it)
    if want_h:
        h_ref = next(it)
    if want_logits:
        lg_ref = next(it)

    x = x_ref[...]
    if has_y:
        y = y_ref[...].astype(F32)
        x = x + gate_ref[0] * _rms(y, gpost_ref[0])
        xo_ref[...] = x
    if want_h:
        h = _rms(x, gpre_ref[0]) * (1.0 + sc_ref[0]) + sh_ref[0]
        hb = h.astype(BF16)
        h_ref[...] = hb
        if want_logits:
            hl = (h - hb.astype(F32)).astype(BF16)
            lg_ref[...] = (jnp.dot(hb, rwh_ref[...], preferred_element_type=F32)
                           + jnp.dot(hl, rwh_ref[...], preferred_element_type=F32)
                           + jnp.dot(hb, rwl_ref[...], preferred_element_type=F32))


def resid_norm(cfg, x, norm_g, *, y_part=None, h_part=None, router=None, n_rows=None):
    D = cfg.D
    tm = _tile(cfg.Tc, 128)
    count = n_rows if n_rows is not None else cfg.R
    assert count % tm == 0
    nlat_t = cfg.n_lat // tm
    tpb = cfg.T // tm
    has_y, want_h, want_logits = y_part is not None, h_part is not None, router is not None

    def set_of(i):
        return jnp.where(i < nlat_t, i // tpb, cfg.B)

    row_spec = lambda w: pl.BlockSpec((tm, w), lambda i: (i, 0))
    mod_spec = lambda k: pl.BlockSpec((1, 1, D), lambda i: (set_of(i) * 6 + k, 0, 0))
    g_spec = lambda layer, k: pl.BlockSpec((1, 1, D), lambda i: (layer * 4 + k, 0, 0))

    args, in_specs, out_shape, out_specs = [x], [row_spec(D)], [], []
    if has_y:
        y, mod, layer, gate_k, gpost_k = y_part
        args += [y, mod, norm_g]
        in_specs += [row_spec(D), mod_spec(gate_k), g_spec(layer, gpost_k)]
    if want_h:
        mod, layer, gpre_k, sc_k, sh_k = h_part
        args += [norm_g, mod, mod]
        in_specs += [g_spec(layer, gpre_k), mod_spec(sc_k), mod_spec(sh_k)]
    if want_logits:
        args += [router[0], router[1]]
        in_specs += [pl.BlockSpec((D, LANES), lambda i: (0, 0))] * 2
    if has_y:
        out_shape.append(jax.ShapeDtypeStruct((count, D), F32))
        out_specs.append(row_spec(D))
    if want_h:
        out_shape.append(jax.ShapeDtypeStruct((count, D), BF16))
        out_specs.append(row_spec(D))
    if want_logits:
        out_shape.append(jax.ShapeDtypeStruct((count, LANES), F32))
        out_specs.append(row_spec(LANES))

    outs = pl.pallas_call(
        functools.partial(_resid_norm_kernel, has_y=has_y, want_h=want_h, want_logits=want_logits),
        grid=(count // tm,),
        in_specs=in_specs,
        out_specs=out_specs,
        out_shape=out_shape,
        compiler_params=_cparams(("parallel",)),
        name="resid_norm",
    )(*args)
    outs = list(outs)
    x_new = outs.pop(0) if has_y else None
    h = outs.pop(0) if want_h else None
    lg = outs.pop(0) if want_logits else None
    return x_new, h, lg


def _mm_kernel(x_ref, w_ref, o_ref):
    o_ref[...] = jnp.dot(x_ref[...], w_ref[...], preferred_element_type=F32).astype(o_ref.dtype)


def matmul(x, w, out_dtype, n_rows=None, n_cols=None, col0=0, tm_pref=512, tn_pref=512):
    K = x.shape[1]
    M = n_rows if n_rows is not None else x.shape[0]
    N = n_cols if n_cols is not None else w.shape[1]
    tm, tn = _tile(M, tm_pref), _tile(N, tn_pref)
    assert col0 % tn == 0
    c0 = col0 // tn
    return pl.pallas_call(
        _mm_kernel,
        grid=(M // tm, N // tn),
        in_specs=[pl.BlockSpec((tm, K), lambda i, j: (i, 0)), pl.BlockSpec((K, tn), lambda i, j: (0, j + c0))],
        out_specs=pl.BlockSpec((tm, tn), lambda i, j: (i, j)),
        out_shape=jax.ShapeDtypeStruct((M, N), out_dtype),
        compiler_params=_cparams(("parallel", "arbitrary")),
        name="matmul",
    )(x, w)


def _qkv_kernel(x_ref, w_ref, cos_ref, sin_ref, o_ref, *, nlat_t, q_tiles, qk_tiles, qscale):
    i, j = pl.program_id(0), pl.program_id(1)
    acc = jnp.dot(x_ref[...], w_ref[...], preferred_element_type=F32)
    acc = acc * jnp.where(j < q_tiles, qscale, 1.0).astype(F32)
    rope = jnp.logical_and(i < nlat_t, j < qk_tiles)

    @pl.when(rope)
    def _():
        tn = acc.shape[1]
        up = pltpu.roll(acc, tn - 32, 1)
        dn = pltpu.roll(acc, 32, 1)
        lane = lax.broadcasted_iota(jnp.int32, acc.shape, 1)
        swap = jnp.where((lane & 63) < 32, up, dn)
        cos = _lane_tile(cos_ref[...], tn // LANES)
        sin = _lane_tile(sin_ref[...], tn // LANES)
        o_ref[...] = (acc * cos + swap * sin).astype(o_ref.dtype)

    @pl.when(jnp.logical_not(rope))
    def _():
        o_ref[...] = acc.astype(o_ref.dtype)


def qkv_proj(cfg, h, w, cos_t, sin_t):
    R, D = h.shape
    N = w.shape[1]
    tm = _tile(cfg.Tc, 512)
    tn = _tile(D, 512)
    nlat_t = cfg.n_lat // tm
    tpb = cfg.T // tm
    qscale = cfg.da_dim ** -0.5 * math.log2(math.e)
    tbl = pl.BlockSpec((tm, LANES), lambda i, j: (i % tpb, 0))
    return pl.pallas_call(
        functools.partial(_qkv_kernel, nlat_t=nlat_t, q_tiles=D // tn, qk_tiles=2 * D // tn, qscale=qscale),
        grid=(R // tm, N // tn),
        in_specs=[pl.BlockSpec((tm, D), lambda i, j: (i, 0)), pl.BlockSpec((D, tn), lambda i, j: (0, j)), tbl, tbl],
        out_specs=pl.BlockSpec((tm, tn), lambda i, j: (i, j)),
        out_shape=jax.ShapeDtypeStruct((R, N), BF16),
        compiler_params=_cparams(("parallel", "arbitrary")),
        name="qkv_proj",
    )(h, w, cos_t, sin_t)


def rope_tables(cfg):
    half = cfg.da_dim // 2
    t = jnp.arange(cfg.T)
    row = (t // cfg.grid_w).astype(F32)
    col = (t % cfg.grid_w).astype(F32)
    inv = ROPE_BASE ** (-jnp.arange(0, half, 2, dtype=F32) / half)
    ar, ac = row[:, None] * inv, col[:, None] * inv
    cos_t = jnp.concatenate([jnp.cos(ar), jnp.cos(ar), jnp.cos(ac), jnp.cos(ac)], axis=1)
    sin_t = jnp.concatenate([-jnp.sin(ar), jnp.sin(ar), -jnp.sin(ac), jnp.sin(ac)], axis=1)
    return cos_t, sin_t


def _attn_kernel(*refs, tk, n_lat, lam_init, dd):
    if n_lat:
        lam_ref, g_ref, q_ref, kc_ref, vc_ref, kl_ref, vl_ref, o_ref, acc_ref, m_ref, l_ref = refs
    else:
        lam_ref, g_ref, q_ref, kc_ref, vc_ref, o_ref, acc_ref, m_ref, l_ref = refs
    d = dd // 2
    q = q_ref[...]

    m_ref[...] = jnp.full(m_ref.shape, -jnp.inf, F32)
    l_ref[...] = jnp.zeros(l_ref.shape, F32)
    acc_ref[...] = jnp.zeros(acc_ref.shape, F32)

    def step(k, v):
        n = k.shape[0]
        for p in range(2):
            s = lax.dot_general(q[:, p * d:(p + 1) * d], k[:, p * d:(p + 1) * d], (((1,), (1,)), ((), ())),
                                preferred_element_type=F32)
            m_prev = m_ref[p]
            m_next = jnp.maximum(m_prev, jnp.max(s, axis=1)[:, None])
            pr = jnp.exp2(s - _lane_tile(m_next, n // LANES))
            alpha = jnp.exp2(m_prev - m_next)
            l_ref[p] = alpha * l_ref[p] + jnp.sum(pr, axis=1)[:, None]
            m_ref[p] = m_next
            acc_ref[p] = acc_ref[p] * _lane_tile(alpha, dd // LANES) + jnp.dot(
                pr.astype(BF16), v, preferred_element_type=F32)

    step(kc_ref[...], vc_ref[...])
    if n_lat:
        def body(j, carry):
            off = pl.multiple_of(j * tk, tk)
            step(kl_ref[pl.ds(off, tk), :], vl_ref[pl.ds(off, tk), :])
            return carry
        lax.fori_loop(0, n_lat, body, 0)

    lv = lam_ref[...]
    lam = (jnp.exp(jnp.sum(lv[0:1] * lv[1:2], axis=1, keepdims=True))
           - jnp.exp(jnp.sum(lv[2:3] * lv[3:4], axis=1, keepdims=True)) + lam_init)
    o1 = acc_ref[0] / _lane_tile(l_ref[0], dd // LANES)
    o2 = acc_ref[1] / _lane_tile(l_ref[1], dd // LANES)
    o = o1 - lam * o2
    o = _rms(o, g_ref[...]) * (1.0 - lam_init)
    o_ref[...] = o.astype(o_ref.dtype)


def diff_attention(cfg, qkv, lam_vecs, subln_g, lam_init, *, latent):
    D, B, T, Tc, H = cfg.D, cfg.B, cfg.T, cfg.Tc, cfg.da_heads
    dd = 2 * cfg.da_dim
    assert T % Tc == 0 and dd % LANES == 0 and Tc % LANES == 0
    ctx_blk0 = cfg.n_lat // Tc
    if latent:
        tq, tk = _tile(T, 512), _tile(T, 512)
        nq, n_lat = T // tq, T // tk
        q_row = lambda b, i: b * nq + i
    else:
        tq, tk, nq, n_lat = Tc, 0, 1, 0
        q_row = lambda b, i: ctx_blk0 + b
    small = lambda shape: pl.BlockSpec(shape, lambda b, h, i: (0, 0))
    in_specs = [
        small((4, cfg.da_dim)),
        small((1, dd)),
        pl.BlockSpec((tq, dd), lambda b, h, i: (q_row(b, i), h)),
        pl.BlockSpec((Tc, dd), lambda b, h, i: (ctx_blk0 + b, H + h)),
        pl.BlockSpec((Tc, dd), lambda b, h, i: (ctx_blk0 + b, 2 * H + h)),
    ]
    args = [lam_vecs, subln_g, qkv, qkv, qkv]
    if latent:
        in_specs += [pl.BlockSpec((T, dd), lambda b, h, i: (b, H + h)),
                     pl.BlockSpec((T, dd), lambda b, h, i: (b, 2 * H + h))]
        args += [qkv, qkv]
    n_out = B * (T if latent else Tc)
    return pl.pallas_call(
        functools.partial(_attn_kernel, tk=tk, n_lat=n_lat, lam_init=lam_init, dd=dd),
        grid=(B, H, nq),
        in_specs=in_specs,
        out_specs=pl.BlockSpec((tq, dd), lambda b, h, i: (b * nq + i, h)),
        out_shape=jax.ShapeDtypeStruct((n_out, D), BF16),
        scratch_shapes=[pltpu.VMEM((2, tq, dd), F32), pltpu.VMEM((2, tq, LANES), F32),
                        pltpu.VMEM((2, tq, LANES), F32)],
        compiler_params=_cparams(("parallel", "parallel", "arbitrary")),
        name="diff_attn_lat" if latent else "diff_attn_ctx",
    )(*args)


def _time_cumsum(x, reverse):
    L = x.shape[0]
    row = lax.broadcasted_iota(jnp.int32, x.shape, 0)
    sh = 1
    while sh < L:
        if reverse:
            x = x + jnp.where(row < L - sh, pltpu.roll(x, L - sh, 0), 0.0)
        else:
            x = x + jnp.where(row >= sh, pltpu.roll(x, sh, 0), 0.0)
        sh *= 2
    return x


def _mlstm_kernel(*refs, H, dqk, dv, L, reverse, finish, gate_off):
    if finish:
        (q_ref, k_ref, v_ref, gt_ref, bias_ref, hf_ref, og_ref, ng_ref, o_ref, c_ref, n_ref, m_ref) = refs
    else:
        (q_ref, k_ref, v_ref, gt_ref, bias_ref, o_ref, c_ref, n_ref, m_ref) = refs

    @pl.when(pl.program_id(1) == 0)
    def _():
        c_ref[...] = jnp.zeros(c_ref.shape, F32)
        n_ref[...] = jnp.zeros(n_ref.shape, F32)
        m_ref[...] = jnp.zeros(m_ref.shape, F32)

    gates = GATE_CAP * jnp.tanh((gt_ref[...] + bias_ref[...]) * (1.0 / GATE_CAP))
    ig = pltpu.roll(gates, LANES - gate_off, 1) if gate_off else gates
    fg = pltpu.roll(gates, LANES - gate_off - H, 1)
    lf = jnp.minimum(fg, 0.0) - jnp.log1p(jnp.exp(-jnp.abs(fg)))
    bc = _time_cumsum(lf, reverse)
    m_prev = m_ref[0:1, :]
    a_all = bc + m_prev
    b_last = bc[0:1, :] if reverse else bc[L - 1:L, :]
    g_all = b_last - bc + ig
    m_new = jnp.maximum(b_last + m_prev, jnp.max(g_all, axis=0, keepdims=True))
    decay = jnp.exp(b_last + m_prev - m_new)
    eg = jnp.exp(g_all - m_new)
    xt = (ig - bc).T
    m_ref[...] = jnp.broadcast_to(m_new, m_ref.shape)

    li = lax.broadcasted_iota(jnp.int32, (L, L), 0)
    si = lax.broadcasted_iota(jnp.int32, (L, L), 1)
    tri = (si >= li) if reverse else (si <= li)
    qscale = dqk ** -0.5

    for h in range(H):
        q = (q_ref[:, h * dqk:(h + 1) * dqk].astype(F32) * qscale).astype(BF16)
        k = k_ref[:, h * dqk:(h + 1) * dqk]
        v = v_ref[:, h * dv:(h + 1) * dv]
        ct = c_ref[h]
        nv = n_ref[h]
        a_h = a_all[:, h:h + 1]
        dmat = jnp.where(tri, bc[:, h:h + 1] + xt[h:h + 1, :], -jnp.inf)
        m_t = jnp.maximum(a_h, jnp.max(dmat, axis=1, keepdims=True))
        qk = lax.dot_general(q, k, (((1,), (1,)), ((), ())), preferred_element_type=F32)
        w = jnp.exp(dmat - m_t) * qk
        inter = jnp.exp(a_h - m_t)
        num = inter * jnp.dot(q, ct.astype(BF16), preferred_element_type=F32) + jnp.dot(
            w.astype(BF16), v, preferred_element_type=F32)
        qn = jnp.sum(q.astype(F32) * nv, axis=1, keepdims=True)
        den = inter * qn + jnp.sum(w, axis=1, keepdims=True)
        hout = num / jnp.maximum(jnp.abs(den), jnp.exp(-m_t))

        dec_h = decay[:, h:h + 1]
        wk = eg[:, h:h + 1] * k.astype(F32)
        c_ref[h] = dec_h * ct + lax.dot_general(wk.astype(BF16), v, (((0,), (0,)), ((), ())),
                                                preferred_element_type=F32)
        n_ref[h] = dec_h * nv + jnp.sum(wk, axis=0, keepdims=True)

        if finish:
            hs = hout + hf_ref[:, h * dv:(h + 1) * dv]
            hn = _rms(hs, ng_ref[:, h * dv:(h + 1) * dv]).astype(BF16)
            og = og_ref[:, h * dv:(h + 1) * dv]
            o_ref[:, h * dv:(h + 1) * dv] = (hn * jax.nn.sigmoid(og)).astype(o_ref.dtype)
        else:
            o_ref[:, h * dv:(h + 1) * dv] = hout.astype(o_ref.dtype)


def mlstm_scan(cfg, proj, gates, bias, *, reverse, hf=None, norm_g=None):
    D, B, T, Tc, H, L = cfg.D, cfg.B, cfg.T, cfg.Tc, cfg.ml_heads, cfg.chunk
    dqk, dv = cfg.ml_qk, cfg.ml_v
    ncc, ncl = Tc // L, T // L
    nc = ncc + ncl
    finish = hf is not None
    qw = H * dqk
    assert D % qw == 0 and (H * dv) == D

    def rb(b, c):
        if reverse:
            return jnp.where(c < ncc, cfg.n_lat // L + b * ncc + (ncc - 1 - c), b * ncl + (nc - 1 - c))
        return jnp.where(c < ncc, cfg.n_lat // L + b * ncc + c, b * ncl + (c - ncc))

    in_specs = [
        pl.BlockSpec((L, qw), lambda b, c: (rb(b, c), 0)),
        pl.BlockSpec((L, qw), lambda b, c: (rb(b, c), 1)),
        pl.BlockSpec((L, D), lambda b, c: (rb(b, c), 2 * qw // D)),
        pl.BlockSpec((L, LANES), lambda b, c: (rb(b, c), 0)),
        pl.BlockSpec((1, LANES), lambda b, c: (0, 0)),
    ]
    args = [proj, proj, proj, gates, bias]
    if finish:
        in_specs += [
            pl.BlockSpec((L, D), lambda b, c: (rb(b, c), 0)),
            pl.BlockSpec((L, D), lambda b, c: (rb(b, c), 2 * qw // D + 1)),
            pl.BlockSpec((1, D), lambda b, c: (0, 0)),
        ]
        args += [hf, proj, norm_g]
    return pl.pallas_call(
        functools.partial(_mlstm_kernel, H=H, dqk=dqk, dv=dv, L=L, reverse=reverse, finish=finish,
                          gate_off=2 * H if reverse else 0),
        grid=(B, nc),
        in_specs=in_specs,
        out_specs=pl.BlockSpec((L, D), lambda b, c: (rb(b, c), 0)),
        out_shape=jax.ShapeDtypeStruct((cfg.R, D), BF16 if finish else F32),
        scratch_shapes=[pltpu.VMEM((H, dqk, dv), F32), pltpu.VMEM((H, 1, dqk), F32), pltpu.VMEM((8, LANES), F32)],
        compiler_params=_cparams(("parallel", "arbitrary")),
        name="mlstm_bwd" if reverse else "mlstm_fwd",
    )(*args)


def _moe_kernel(x_ref, g_ref, wg_ref, wu_ref, wd_ref, o_ref, acc_ref):
    f = pl.program_id(2)
    x = x_ref[...]
    a = jnp.dot(x, wg_ref[...], preferred_element_type=F32)
    u = jnp.dot(x, wu_ref[...], preferred_element_type=F32)
    hid = (a * jax.nn.sigmoid(a) * u).astype(BF16)
    part = jnp.dot(hid, wd_ref[...], preferred_element_type=F32)

    @pl.when(f == 0)
    def _():
        acc_ref[...] = part

    @pl.when(f > 0)
    def _():
        acc_ref[...] += part

    @pl.when(f == pl.num_programs(2) - 1)
    def _():
        o_ref[...] = (acc_ref[...] * g_ref[...]).astype(o_ref.dtype)


def moe_ffn(cfg, xe, ge, w_gate, w_up, w_down):
    E, P, D = xe.shape
    F = cfg.F
    tp = P // 4 if (P // 4) % 16 == 0 else P
    fk = _tile(F, 256)
    return pl.pallas_call(
        _moe_kernel,
        grid=(E, P // tp, F // fk),
        in_specs=[
            pl.BlockSpec((None, tp, D), lambda e, p, f: (e, p, 0)),
            pl.BlockSpec((None, tp, 1), lambda e, p, f: (e, p, 0)),
            pl.BlockSpec((None, D, fk), lambda e, p, f: (e, 0, f)),
            pl.BlockSpec((None, D, fk), lambda e, p, f: (e, 0, f)),
            pl.BlockSpec((None, fk, D), lambda e, p, f: (e, f, 0)),
        ],
        out_specs=pl.BlockSpec((None, tp, D), lambda e, p, f: (e, p, 0)),
        out_shape=jax.ShapeDtypeStruct((E, P, D), BF16),
        scratch_shapes=[pltpu.VMEM((tp, D), F32)],
        compiler_params=_cparams(("parallel", "parallel", "arbitrary")),
        name="moe_ffn",
    )(xe, ge, w_gate, w_up, w_down)


def expert_choice_moe(cfg, h, logits, w_gate, w_up, w_down, with_ctx):
    B, T, Tc, E = cfg.B, cfg.T, cfg.Tc, cfg.E
    n_rows = cfg.R if with_ctx else cfg.n_lat
    aff = jax.nn.softmax(logits[:n_rows, :E], axis=-1)
    sets = [(b * T, T) for b in range(B)]
    if with_ctx:
        sets += [(cfg.n_lat + b * Tc, Tc) for b in range(B)]
    rows, gates = [], []
    for start, n in sets:
        cap = max(1, (cfg.cap_factor * n) // E)
        g, idx = lax.top_k(aff[start:start + n].T, cap)
        rows.append(idx + start)
        gates.append(g)
    rows = jnp.concatenate(rows, axis=1)
    gates = jnp.concatenate(gates, axis=1)
    xe = h[rows]
    ye = moe_ffn(cfg, xe, gates[..., None], w_gate, w_up, w_down)
    return jnp.zeros((n_rows, cfg.D), F32).at[rows.reshape(-1)].add(ye.reshape(-1, cfg.D).astype(F32))


def _forward(cfg, x, c, ctx, c_ctx, ada_w, ada_b, norm_g, attn_w_in, attn_w_out, attn_lambda, attn_subln_g,
             mlstm_w_in, mlstm_b_gates, mlstm_norm_g, mlstm_w_out, router_w, expert_w_gate, expert_w_up,
             expert_w_down):
    D, B, T, Tc, E = cfg.D, cfg.B, cfg.T, cfg.Tc, cfg.E
    depth = ada_w.shape[0]
    cos_t, sin_t = rope_tables(cfg)
    cvec = jnp.concatenate([c, c_ctx[None, :], jnp.zeros((8 - B - 1, D), F32)], axis=0)
    ada_b3 = ada_b.reshape(depth, 1, 6 * D)
    norm_g3 = norm_g.reshape(depth * 4, 1, D)
    X = jnp.concatenate([x.reshape(B * T, D), ctx.reshape(B * Tc, D)], axis=0)

    mods = [adaln(cvec, ada_w, ada_b3, i)[:B + 1].reshape((B + 1) * 6, 1, D) for i in range(depth)]
    rw = jnp.pad(router_w, ((0, 0), (0, 0), (0, LANES - E)))
    rw_hi = rw.astype(BF16)
    rw_lo = (rw - rw_hi.astype(F32)).astype(BF16)

    _, h, _ = resid_norm(cfg, X, norm_g3, h_part=(mods[0], 0, 0, 1, 0))
    for i in range(depth):
        last = i == depth - 1
        mod = mods[i]
        j = i // 2
        n_rows = cfg.n_lat if last else cfg.R
        if i % 2 == 0:
            lam_init = 0.8 - 0.6 * math.exp(-0.3 * i)
            qkv = qkv_proj(cfg, h, attn_w_in[j].astype(BF16), cos_t, sin_t)
            g2 = attn_subln_g[j].reshape(1, -1)
            o = diff_attention(cfg, qkv, attn_lambda[j], g2, lam_init, latent=True)
            if not last:
                o = jnp.concatenate([o, diff_attention(cfg, qkv, attn_lambda[j], g2, lam_init, latent=False)], 0)
            y = matmul(o, attn_w_out[j].astype(BF16), BF16)
        else:
            H = cfg.ml_heads
            w_in = mlstm_w_in[j]
            nmain = w_in.shape[1] - 4 * H
            proj = matmul(h, w_in[:, :nmain].astype(BF16), BF16)
            wg = jnp.pad(w_in[:, nmain:], ((0, 0), (0, LANES - 4 * H))).astype(BF16)
            gates = matmul(h, wg, F32)
            bias = jnp.pad(mlstm_b_gates[j].reshape(1, 4 * H), ((0, 0), (0, LANES - 4 * H)))
            hf = mlstm_scan(cfg, proj, gates, bias, reverse=False)
            hn = mlstm_scan(cfg, proj, gates, bias, reverse=True, hf=hf, norm_g=mlstm_norm_g[j].reshape(1, D))
            y = matmul(hn, mlstm_w_out[j].astype(BF16), BF16, n_rows=n_rows)
        X, h, lg = resid_norm(cfg, X, norm_g3, y_part=(y, mod, i, 2, 1), h_part=(mod, i, 2, 4, 3),
                              router=(rw_hi[i], rw_lo[i]), n_rows=n_rows)
        mo = expert_choice_moe(cfg, h, lg, expert_w_gate[i].astype(BF16), expert_w_up[i].astype(BF16),
                               expert_w_down[i].astype(BF16), with_ctx=not last)
        if last:
            X, _, _ = resid_norm(cfg, X, norm_g3, y_part=(mo, mod, i, 5, 3), n_rows=n_rows)
        else:
            X, h, _ = resid_norm(cfg, X, norm_g3, y_part=(mo, mod, i, 5, 3), h_part=(mods[i + 1], i + 1, 0, 1, 0))
    return X.reshape(B, T, D)


def kernel(x, c, ctx, c_ctx, ada_w, ada_b, norm_g, attn_w_in, attn_w_out, attn_lambda, attn_subln_g, mlstm_w_in,
           mlstm_b_gates, mlstm_norm_g, mlstm_w_out, router_w, expert_w_gate, expert_w_up, expert_w_down):
    return _forward(FULL_CFG, x, c, ctx, c_ctx, ada_w, ada_b, norm_g, attn_w_in, attn_w_out, attn_lambda,
                    attn_subln_g, mlstm_w_in, mlstm_b_gates, mlstm_norm_g, mlstm_w_out, router_w, expert_w_gate,
                    expert_w_up, expert_w_down)
```

```python
import functools
import math
from typing import NamedTuple

import jax
import jax.numpy as jnp
from jax import lax
from jax.experimental import pallas as pl
from jax.experimental.pallas import tpu as pltpu

BF16 = jnp.bfloat16
F32 = jnp.float32

EPS = 1e-6
GATE_CAP = 15.0
ROPE_BASE = 10000.0
LANES = 128
MIB = 1024 * 1024
VMEM_LIMIT = 56 * MIB


class Cfg(NamedTuple):
    D: int
    B: int
    T: int
    Tc: int
    grid_w: int
    da_heads: int
    da_dim: int
    ml_heads: int
    E: int
    F: int
    chunk: int
    cap_factor: int

    @property
    def n_lat(self):
        return self.B * self.T

    @property
    def n_ctx(self):
        return self.B * self.Tc

    @property
    def R(self):
        return self.n_lat + self.n_ctx

    @property
    def ml_qk(self):
        return self.D // 2 // self.ml_heads

    @property
    def ml_v(self):
        return self.D // self.ml_heads


FULL_CFG = Cfg(D=4096, B=2, T=8192, Tc=256, grid_w=64, da_heads=16, da_dim=128, ml_heads=8, E=16, F=1024,
               chunk=64, cap_factor=2)


def _cparams(sem, vmem=VMEM_LIMIT):
    return pltpu.CompilerParams(dimension_semantics=sem, vmem_limit_bytes=vmem)


def _tile(n, pref):
    t = min(n, pref)
    while n % t:
        t //= 2
    return t


def _lane_tile(x, reps):
    return x if reps == 1 else jnp.concatenate([x] * reps, axis=1)


def _adaln_kernel(c_ref, w_ref, b_ref, o_ref):
    c = c_ref[...]
    s = (c * jax.nn.sigmoid(c)).astype(BF16)
    o_ref[...] = jnp.dot(s, w_ref[...].astype(BF16), preferred_element_type=F32) + b_ref[...]


def adaln(cvec, ada_w, ada_b3, layer):
    D = cvec.shape[1]
    N = ada_w.shape[2]
    tn = _tile(N, 512)
    return pl.pallas_call(
        _adaln_kernel,
        grid=(N // tn,),
        in_specs=[
            pl.BlockSpec((8, D), lambda j: (0, 0)),
            pl.BlockSpec((None, D, tn), lambda j: (layer, 0, j)),
            pl.BlockSpec((None, 1, tn), lambda j: (layer, 0, j)),
        ],
        out_specs=pl.BlockSpec((8, tn), lambda j: (0, j)),
        out_shape=jax.ShapeDtypeStruct((8, N), F32),
        compiler_params=_cparams(("parallel",)),
        name="adaln",
    )(cvec, ada_w, ada_b3)


def _rms(x, g):
    return x * lax.rsqrt(jnp.mean(x * x, axis=-1, keepdims=True) + EPS) * g


def _resid_norm_kernel(*refs, has_y, want_h, want_logits):
    it = iter(refs)
    x_ref = next(it)
    if has_y:
        y_ref, gate_ref, gpost_ref = next(it), next(it), next(it)
    if want_h:
        gpre_ref, sc_ref, sh_ref = next(it), next(it), next(it)
    if want_logits:
        rwh_ref, rwl_ref = next(it), next(it)
    if has_y:
        xo_ref = next(it)
    if want_h:
        h_ref = next(it)
    if want_logits:
        lg_ref = next(it)

    x = x_ref[...]
    if has_y:
        y = y_ref[...].astype(F32)
        x = x + gate_ref[0] * _rms(y, gpost_ref[0])
        xo_ref[...] = x
    if want_h:
        h = _rms(x, gpre_ref[0]) * (1.0 + sc_ref[0]) + sh_ref[0]
        hb = h.astype(BF16)
        h_ref[...] = hb
        if want_logits:
            hl = (h - hb.astype(F32)).astype(BF16)
            lg_ref[...] = (jnp.dot(hb, rwh_ref[...], preferred_element_type=F32)
                           + jnp.dot(hl, rwh_ref[...], preferred_element_type=F32)
                           + jnp.dot(hb, rwl_ref[...], preferred_element_type=F32))


def resid_norm(cfg, x, norm_g, *, y_part=None, h_part=None, router=None, n_rows=None):
    D = cfg.D
    tm = _tile(cfg.Tc, 128)
    count = n_rows if n_rows is not None else cfg.R
    assert count % tm == 0
    nlat_t = cfg.n_lat // tm
    tpb = cfg.T // tm
    has_y, want_h, want_logits = y_part is not None, h_part is not None, router is not None

    def set_of(i):
        return jnp.where(i < nlat_t, i // tpb, cfg.B)

    row_spec = lambda w: pl.BlockSpec((tm, w), lambda i: (i, 0))
    mod_spec = lambda k: pl.BlockSpec((1, 1, D), lambda i: (set_of(i) * 6 + k, 0, 0))
    g_spec = lambda layer, k: pl.BlockSpec((1, 1, D), lambda i: (layer * 4 + k, 0, 0))

    args, in_specs, out_shape, out_specs = [x], [row_spec(D)], [], []
    if has_y:
        y, mod, layer, gate_k, gpost_k = y_part
        args += [y, mod, norm_g]
        in_specs += [row_spec(D), mod_spec(gate_k), g_spec(layer, gpost_k)]
    if want_h:
        mod, layer, gpre_k, sc_k, sh_k = h_part
        args += [norm_g, mod, mod]
        in_specs += [g_spec(layer, gpre_k), mod_spec(sc_k), mod_spec(sh_k)]
    if want_logits:
        args += [router[0], router[1]]
        in_specs += [pl.BlockSpec((D, LANES), lambda i: (0, 0))] * 2
    if has_y:
        out_shape.append(jax.ShapeDtypeStruct((count, D), F32))
        out_specs.append(row_spec(D))
    if want_h:
        out_shape.append(jax.ShapeDtypeStruct((count, D), BF16))
        out_specs.append(row_spec(D))
    if want_logits:
        out_shape.append(jax.ShapeDtypeStruct((count, LANES), F32))
        out_specs.append(row_spec(LANES))

    outs = pl.pallas_call(
        functools.partial(_resid_norm_kernel, has_y=has_y, want_h=want_h, want_logits=want_logits),
        grid=(count // tm,),
        in_specs=in_specs,
        out_specs=out_specs,
        out_shape=out_shape,
        compiler_params=_cparams(("parallel",)),
        name="resid_norm",
    )(*args)
    outs = list(outs)
    x_new = outs.pop(0) if has_y else None
    h = outs.pop(0) if want_h else None
    lg = outs.pop(0) if want_logits else None
    return x_new, h, lg


def _mm_kernel(x_ref, w_ref, o_ref, *, scale_tiles, scale):
    acc = jnp.dot(x_ref[...], w_ref[...], preferred_element_type=F32)
    if scale_tiles:
        acc = acc * jnp.where(pl.program_id(1) < scale_tiles, scale, 1.0).astype(F32)
    o_ref[...] = acc.astype(o_ref.dtype)


def matmul(x, w, out_dtype, n_rows=None, row0=0, scale_cols=0, scale=1.0, tm_pref=512, tn_pref=512):
    K = x.shape[1]
    M = n_rows if n_rows is not None else x.shape[0] - row0
    N = w.shape[1]
    tm, tn = _tile(M, tm_pref), _tile(N, tn_pref)
    assert row0 % tm == 0 and scale_cols % tn == 0
    r0 = row0 // tm
    return pl.pallas_call(
        functools.partial(_mm_kernel, scale_tiles=scale_cols // tn, scale=scale),
        grid=(M // tm, N // tn),
        in_specs=[pl.BlockSpec((tm, K), lambda i, j: (i + r0, 0)), pl.BlockSpec((K, tn), lambda i, j: (0, j))],
        out_specs=pl.BlockSpec((tm, tn), lambda i, j: (i, j)),
        out_shape=jax.ShapeDtypeStruct((M, N), out_dtype),
        compiler_params=_cparams(("parallel", "arbitrary")),
        name="matmul",
    )(x, w)


def _qk_rope_kernel(x_ref, w_ref, cos_ref, sin_ref, o_ref, *, q_tiles, qscale):
    acc = jnp.dot(x_ref[...], w_ref[...], preferred_element_type=F32)
    acc = acc * jnp.where(pl.program_id(1) < q_tiles, qscale, 1.0).astype(F32)
    cos, sin = cos_ref[...], sin_ref[...]
    for g in range(acc.shape[1] // LANES):
        xg = acc[:, g * LANES:(g + 1) * LANES]
        o_ref[:, g * LANES:(g + 1) * LANES] = (xg * cos + pltpu.roll(xg, LANES // 2, 1) * sin).astype(o_ref.dtype)


def qk_proj_rope(cfg, h, w_qk, cos_t, sin_t):
    D = cfg.D
    N = w_qk.shape[1]
    tm = _tile(cfg.T, 512)
    tn = _tile(D, 512)
    tpb = cfg.T // tm
    qscale = cfg.da_dim ** -0.5 * math.log2(math.e)
    tbl = pl.BlockSpec((tm, LANES), lambda i, j: (i % tpb, 0))
    return pl.pallas_call(
        functools.partial(_qk_rope_kernel, q_tiles=D // tn, qscale=qscale),
        grid=(cfg.n_lat // tm, N // tn),
        in_specs=[pl.BlockSpec((tm, D), lambda i, j: (i, 0)), pl.BlockSpec((D, tn), lambda i, j: (0, j)), tbl, tbl],
        out_specs=pl.BlockSpec((tm, tn), lambda i, j: (i, j)),
        out_shape=jax.ShapeDtypeStruct((cfg.n_lat, N), BF16),
        compiler_params=_cparams(("parallel", "arbitrary")),
        name="qk_proj_rope",
    )(h, w_qk, cos_t, sin_t)


def rope_col_order(cfg, w):
    K, N = w.shape
    q = cfg.da_dim // 4
    return w.reshape(K, N // cfg.da_dim, 2, 2, q).transpose(0, 1, 3, 2, 4).reshape(K, N)


def rope_tables(cfg):
    half = cfg.da_dim // 2
    t = jnp.arange(cfg.T)
    row = (t // cfg.grid_w).astype(F32)
    col = (t % cfg.grid_w).astype(F32)
    inv = ROPE_BASE ** (-jnp.arange(0, half, 2, dtype=F32) / half)
    ar, ac = row[:, None] * inv, col[:, None] * inv
    cos_t = jnp.concatenate([jnp.cos(ar), jnp.cos(ac), jnp.cos(ar), jnp.cos(ac)], axis=1)
    sin_t = jnp.concatenate([-jnp.sin(ar), -jnp.sin(ac), jnp.sin(ar), jnp.sin(ac)], axis=1)
    return cos_t, sin_t


def _attn_kernel(*refs, tk, n_lat, lam_init, dd):
    if n_lat:
        (lam_ref, g_ref, q_ref, kc_ref, vc_ref, kl_ref, vl_ref, o_ref, acc_ref, m_ref, l_ref, sa_ref,
         sb_ref) = refs
    else:
        lam_ref, g_ref, q_ref, kc_ref, vc_ref, o_ref, acc_ref, m_ref, l_ref = refs
    d = dd // 2
    q = q_ref[...]

    m_ref[...] = jnp.full(m_ref.shape, -jnp.inf, F32)
    l_ref[...] = jnp.zeros(l_ref.shape, F32)
    acc_ref[...] = jnp.zeros(acc_ref.shape, F32)

    def score(k, p):
        return lax.dot_general(q[:, p * d:(p + 1) * d], k[:, p * d:(p + 1) * d], (((1,), (1,)), ((), ())),
                               preferred_element_type=F32)

    def softmax_pv(s_of, v):
        n = v.shape[0]
        for p in range(2):
            s = s_of(p)
            m_prev = m_ref[p]
            m_next = jnp.maximum(m_prev, jnp.max(s, axis=1)[:, None])
            pr = jnp.exp2(s - _lane_tile(m_next, n // LANES))
            alpha = jnp.exp2(m_prev - m_next)
            l_ref[p] = alpha * l_ref[p] + jnp.sum(pr, axis=1)[:, None]
            m_ref[p] = m_next
            acc_ref[p] = acc_ref[p] * _lane_tile(alpha, dd // LANES) + jnp.dot(
                pr.astype(BF16), v, preferred_element_type=F32)

    kc = kc_ref[...]
    softmax_pv(lambda p: score(kc, p), vc_ref[...])
    if n_lat:
        def chunk(ref, j):
            return ref[pl.ds(pl.multiple_of(j * tk, tk), tk), :]

        def scores_to(s_ref, j):
            k = chunk(kl_ref, j)
            for p in range(2):
                s_ref[p] = score(k, p)

        scores_to(sa_ref, 0)

        def body(jj, carry):
            j0 = 2 * jj
            scores_to(sb_ref, j0 + 1)
            softmax_pv(lambda p: sa_ref[p], chunk(vl_ref, j0))
            scores_to(sa_ref, j0 + 2)
            softmax_pv(lambda p: sb_ref[p], chunk(vl_ref, j0 + 1))
            return carry

        lax.fori_loop(0, n_lat // 2 - 1, body, 0)
        scores_to(sb_ref, n_lat - 1)
        softmax_pv(lambda p: sa_ref[p], chunk(vl_ref, n_lat - 2))
        softmax_pv(lambda p: sb_ref[p], chunk(vl_ref, n_lat - 1))

    lv = lam_ref[...]
    lam = (jnp.exp(jnp.sum(lv[0:1] * lv[1:2], axis=1, keepdims=True))
           - jnp.exp(jnp.sum(lv[2:3] * lv[3:4], axis=1, keepdims=True)) + lam_init)
    o1 = acc_ref[0] / _lane_tile(l_ref[0], dd // LANES)
    o2 = acc_ref[1] / _lane_tile(l_ref[1], dd // LANES)
    o = o1 - lam * o2
    o = _rms(o, g_ref[...]) * (1.0 - lam_init)
    o_ref[...] = o.astype(o_ref.dtype)


def diff_attention(cfg, qk_lat, qk_ctx, v, lam_vecs, subln_g, lam_init, *, latent):
    D, B, T, Tc, H = cfg.D, cfg.B, cfg.T, cfg.Tc, cfg.da_heads
    dd = 2 * cfg.da_dim
    assert T % Tc == 0 and dd % LANES == 0 and Tc % LANES == 0
    ctx_blk0 = cfg.n_lat // Tc
    small = lambda shape: pl.BlockSpec(shape, lambda b, h, i: (0, 0))
    kv_ctx = [pl.BlockSpec((Tc, dd), lambda b, h, i: (b, H + h)),
              pl.BlockSpec((Tc, dd), lambda b, h, i: (ctx_blk0 + b, h))]
    if latent:
        tq, tk = _tile(T, 512), _tile(T // 2, 512)
        nq, n_lat = T // tq, T // tk
        assert n_lat % 2 == 0
        in_specs = [small((4, cfg.da_dim)), small((1, dd)),
                    pl.BlockSpec((tq, dd), lambda b, h, i: (b * nq + i, h))] + kv_ctx + [
            pl.BlockSpec((T, dd), lambda b, h, i: (b, H + h)),
            pl.BlockSpec((T, dd), lambda b, h, i: (b, h))]
        args = [lam_vecs, subln_g, qk_lat, qk_ctx, v, qk_lat, v]
        extra = [pltpu.VMEM((2, tq, tk), F32), pltpu.VMEM((2, tq, tk), F32)]
    else:
        tq, tk, nq, n_lat = Tc, 0, 1, 0
        in_specs = [small((4, cfg.da_dim)), small((1, dd)),
                    pl.BlockSpec((tq, dd), lambda b, h, i: (b, h))] + kv_ctx
        args = [lam_vecs, subln_g, qk_ctx, qk_ctx, v]
        extra = []
    scratch = [pltpu.VMEM((2, tq, dd), F32), pltpu.VMEM((2, tq, LANES), F32), pltpu.VMEM((2, tq, LANES), F32)]
    n_out = B * (T if latent else Tc)
    return pl.pallas_call(
        functools.partial(_attn_kernel, tk=tk, n_lat=n_lat, lam_init=lam_init, dd=dd),
        grid=(B, H, nq),
        in_specs=in_specs,
        out_specs=pl.BlockSpec((tq, dd), lambda b, h, i: (b * nq + i, h)),
        out_shape=jax.ShapeDtypeStruct((n_out, D), BF16),
        scratch_shapes=scratch + extra,
        compiler_params=_cparams(("parallel", "parallel", "arbitrary")),
        name="diff_attn_lat" if latent else "diff_attn_ctx",
    )(*args)


def _time_cumsum(x, reverse):
    L = x.shape[0]
    row = lax.broadcasted_iota(jnp.int32, x.shape, 0)
    sh = 1
    while sh < L:
        if reverse:
            x = x + jnp.where(row < L - sh, pltpu.roll(x, L - sh, 0), 0.0)
        else:
            x = x + jnp.where(row >= sh, pltpu.roll(x, sh, 0), 0.0)
        sh *= 2
    return x


def _mlstm_kernel(*refs, H, dqk, dv, L, reverse, finish, gate_off):
    if finish:
        (q_ref, k_ref, v_ref, gt_ref, bias_ref, hf_ref, og_ref, ng_ref, o_ref, c_ref, n_ref, m_ref) = refs
    else:
        (q_ref, k_ref, v_ref, gt_ref, bias_ref, o_ref, c_ref, n_ref, m_ref) = refs

    @pl.when(pl.program_id(1) == 0)
    def _():
        c_ref[...] = jnp.zeros(c_ref.shape, F32)
        n_ref[...] = jnp.zeros(n_ref.shape, F32)
        m_ref[...] = jnp.zeros(m_ref.shape, F32)

    gates = GATE_CAP * jnp.tanh((gt_ref[...] + bias_ref[...]) * (1.0 / GATE_CAP))
    ig = pltpu.roll(gates, LANES - gate_off, 1) if gate_off else gates
    fg = pltpu.roll(gates, LANES - gate_off - H, 1)
    lf = jnp.minimum(fg, 0.0) - jnp.log1p(jnp.exp(-jnp.abs(fg)))
    bc = _time_cumsum(lf, reverse)
    m_prev = m_ref[0:1, :]
    a_all = bc + m_prev
    b_last = bc[0:1, :] if reverse else bc[L - 1:L, :]
    g_all = b_last - bc + ig
    m_new = jnp.maximum(b_last + m_prev, jnp.max(g_all, axis=0, keepdims=True))
    decay = jnp.exp(b_last + m_prev - m_new)
    eg = jnp.exp(g_all - m_new)
    xt = (ig - bc).T
    m_ref[...] = jnp.broadcast_to(m_new, m_ref.shape)

    li = lax.broadcasted_iota(jnp.int32, (L, L), 0)
    si = lax.broadcasted_iota(jnp.int32, (L, L), 1)
    tri = (si >= li) if reverse else (si <= li)
    qscale = dqk ** -0.5

    for h in range(H):
        q = (q_ref[:, h * dqk:(h + 1) * dqk].astype(F32) * qscale).astype(BF16)
        k = k_ref[:, h * dqk:(h + 1) * dqk]
        v = v_ref[:, h * dv:(h + 1) * dv]
        ct = c_ref[h]
        nv = n_ref[h]
        a_h = a_all[:, h:h + 1]
        dmat = jnp.where(tri, bc[:, h:h + 1] + xt[h:h + 1, :], -jnp.inf)
        m_t = jnp.maximum(a_h, jnp.max(dmat, axis=1, keepdims=True))
        qk = lax.dot_general(q, k, (((1,), (1,)), ((), ())), preferred_element_type=F32)
        w = jnp.exp(dmat - m_t) * qk
        inter = jnp.exp(a_h - m_t)
        num = inter * jnp.dot(q, ct.astype(BF16), preferred_element_type=F32) + jnp.dot(
            w.astype(BF16), v, preferred_element_type=F32)
        qn = jnp.sum(q.astype(F32) * nv, axis=1, keepdims=True)
        den = inter * qn + jnp.sum(w, axis=1, keepdims=True)
        hout = num / jnp.maximum(jnp.abs(den), jnp.exp(-m_t))

        dec_h = decay[:, h:h + 1]
        wk = eg[:, h:h + 1] * k.astype(F32)
        c_ref[h] = dec_h * ct + lax.dot_general(wk.astype(BF16), v, (((0,), (0,)), ((), ())),
                                                preferred_element_type=F32)
        n_ref[h] = dec_h * nv + jnp.sum(wk, axis=0, keepdims=True)

        if finish:
            hs = hout + hf_ref[:, h * dv:(h + 1) * dv]
            hn = _rms(hs, ng_ref[:, h * dv:(h + 1) * dv]).astype(BF16)
            og = og_ref[:, h * dv:(h + 1) * dv]
            o_ref[:, h * dv:(h + 1) * dv] = (hn * jax.nn.sigmoid(og)).astype(o_ref.dtype)
        else:
            o_ref[:, h * dv:(h + 1) * dv] = hout.astype(o_ref.dtype)


def mlstm_scan(cfg, proj, gates, bias, *, reverse, hf=None, norm_g=None):
    D, B, T, Tc, H, L = cfg.D, cfg.B, cfg.T, cfg.Tc, cfg.ml_heads, cfg.chunk
    dqk, dv = cfg.ml_qk, cfg.ml_v
    ncc, ncl = Tc // L, T // L
    nc = ncc + ncl
    finish = hf is not None
    qw = H * dqk
    assert D % qw == 0 and (H * dv) == D

    def rb(b, c):
        if reverse:
            return jnp.where(c < ncc, cfg.n_lat // L + b * ncc + (ncc - 1 - c), b * ncl + (nc - 1 - c))
        return jnp.where(c < ncc, cfg.n_lat // L + b * ncc + c, b * ncl + (c - ncc))

    in_specs = [
        pl.BlockSpec((L, qw), lambda b, c: (rb(b, c), 0)),
        pl.BlockSpec((L, qw), lambda b, c: (rb(b, c), 1)),
        pl.BlockSpec((L, D), lambda b, c: (rb(b, c), 2 * qw // D)),
        pl.BlockSpec((L, LANES), lambda b, c: (rb(b, c), 0)),
        pl.BlockSpec((1, LANES), lambda b, c: (0, 0)),
    ]
    args = [proj, proj, proj, gates, bias]
    if finish:
        in_specs += [
            pl.BlockSpec((L, D), lambda b, c: (rb(b, c), 0)),
            pl.BlockSpec((L, D), lambda b, c: (rb(b, c), 2 * qw // D + 1)),
            pl.BlockSpec((1, D), lambda b, c: (0, 0)),
        ]
        args += [hf, proj, norm_g]
    return pl.pallas_call(
        functools.partial(_mlstm_kernel, H=H, dqk=dqk, dv=dv, L=L, reverse=reverse, finish=finish,
                          gate_off=2 * H if reverse else 0),
        grid=(B, nc),
        in_specs=in_specs,
        out_specs=pl.BlockSpec((L, D), lambda b, c: (rb(b, c), 0)),
        out_shape=jax.ShapeDtypeStruct((cfg.R, D), BF16 if finish else F32),
        scratch_shapes=[pltpu.VMEM((H, dqk, dv), F32), pltpu.VMEM((H, 1, dqk), F32), pltpu.VMEM((8, LANES), F32)],
        compiler_params=_cparams(("parallel", "arbitrary")),
        name="mlstm_bwd" if reverse else "mlstm_fwd",
    )(*args)


def _moe_up_kernel(x_ref, wg_ref, wu_ref, o_ref, wgb_ref, wub_ref):
    @pl.when(pl.program_id(2) == 0)
    def _():
        wgb_ref[...] = wg_ref[...].astype(BF16)
        wub_ref[...] = wu_ref[...].astype(BF16)

    x = x_ref[...]
    a = jnp.dot(x, wgb_ref[...], preferred_element_type=F32)
    u = jnp.dot(x, wub_ref[...], preferred_element_type=F32)
    o_ref[...] = (a * jax.nn.sigmoid(a) * u).astype(o_ref.dtype)


def _row_tile(n, cap):
    best = n
    for t in range(16, min(n, cap) + 1, 16):
        if n % t == 0:
            best = t
    return best


def moe_up(cfg, xe, w_gate, w_up, layer):
    E, P, D = xe.shape
    F = cfg.F
    tp = _row_tile(P, 528)
    fk = _tile(F, 256)
    w_spec = pl.BlockSpec((None, None, D, fk), lambda e, f, p: (layer, e, 0, f))
    return pl.pallas_call(
        _moe_up_kernel,
        grid=(E, F // fk, P // tp),
        in_specs=[pl.BlockSpec((None, tp, D), lambda e, f, p: (e, p, 0)), w_spec, w_spec],
        out_specs=pl.BlockSpec((None, tp, fk), lambda e, f, p: (e, p, f)),
        out_shape=jax.ShapeDtypeStruct((E, P, F), BF16),
        scratch_shapes=[pltpu.VMEM((D, fk), BF16), pltpu.VMEM((D, fk), BF16)],
        compiler_params=_cparams(("parallel", "parallel", "arbitrary")),
        name="moe_up",
    )(xe, w_gate, w_up)


def _moe_down_kernel(rows_ref, hid_ref, g_ref, wd_ref, acc_in_ref, acc_ref, wdb_ref, buf_ref, sem, *, tp, P):
    del acc_in_ref
    e, p = pl.program_id(0), pl.program_id(1)
    base = e * P + p * tp

    @pl.when(p == 0)
    def _():
        wdb_ref[...] = wd_ref[...].astype(BF16)

    def row_in(r):
        return pltpu.make_async_copy(acc_ref.at[pl.ds(rows_ref[base + r], 1)], buf_ref.at[pl.ds(r, 1)], sem.at[0])

    def row_out(r):
        return pltpu.make_async_copy(buf_ref.at[pl.ds(r, 1)], acc_ref.at[pl.ds(rows_ref[base + r], 1)], sem.at[1])

    def all_rows(sem_idx):
        return pltpu.make_async_copy(acc_ref.at[pl.ds(0, tp)], buf_ref, sem.at[sem_idx])

    def start_in(r, c):
        row_in(r).start()
        return c

    def start_out(r, c):
        row_out(r).start()
        return c

    lax.fori_loop(0, tp, start_in, 0)
    y = jnp.dot(hid_ref[...], wdb_ref[...], preferred_element_type=F32) * g_ref[...]
    all_rows(0).wait()
    buf_ref[...] += y
    lax.fori_loop(0, tp, start_out, 0)
    all_rows(1).wait()


def moe_down_scatter(cfg, rows, hid, ge, w_down, layer, n_rows):
    E, P, F = hid.shape
    D = cfg.D
    tp = _row_tile(P, 352)
    grid_spec = pltpu.PrefetchScalarGridSpec(
        num_scalar_prefetch=1,
        grid=(E, P // tp),
        in_specs=[
            pl.BlockSpec((None, tp, F), lambda e, p, rows: (e, p, 0)),
            pl.BlockSpec((None, tp, 1), lambda e, p, rows: (e, p, 0)),
            pl.BlockSpec((None, None, F, D), lambda e, p, rows: (layer, e, 0, 0), pipeline_mode=pl.Buffered(1)),
            pl.BlockSpec(memory_space=pl.ANY),
        ],
        out_specs=pl.BlockSpec(memory_space=pl.ANY),
        scratch_shapes=[pltpu.VMEM((F, D), BF16), pltpu.VMEM((tp, D), F32), pltpu.SemaphoreType.DMA((2,))],
    )
    return pl.pallas_call(
        functools.partial(_moe_down_kernel, tp=tp, P=P),
        grid_spec=grid_spec,
        out_shape=jax.ShapeDtypeStruct((n_rows, D), F32),
        input_output_aliases={4: 0},
        compiler_params=_cparams(("arbitrary", "arbitrary")),
        name="moe_down_scatter",
    )(rows.reshape(-1), hid, ge, w_down, jnp.zeros((n_rows, D), F32))


def expert_choice_moe(cfg, h, logits, w_gate, w_up, w_down, layer, with_ctx):
    B, T, Tc, E = cfg.B, cfg.T, cfg.Tc, cfg.E
    n_rows = cfg.R if with_ctx else cfg.n_lat
    aff = jax.nn.softmax(logits[:n_rows, :E], axis=-1)
    sets = [(b * T, T) for b in range(B)]
    if with_ctx:
        sets += [(cfg.n_lat + b * Tc, Tc) for b in range(B)]
    rows, gates = [], []
    for start, n in sets:
        cap = max(1, (cfg.cap_factor * n) // E)
        g, idx = lax.top_k(aff[start:start + n].T, cap)
        rows.append(idx + start)
        gates.append(g)
    rows = jnp.concatenate(rows, axis=1)
    gates = jnp.concatenate(gates, axis=1)
    xe = h[rows]
    hid = moe_up(cfg, xe, w_gate, w_up, layer)
    return moe_down_scatter(cfg, rows, hid, gates[..., None], w_down, layer, n_rows)


def _forward(cfg, x, c, ctx, c_ctx, ada_w, ada_b, norm_g, attn_w_in, attn_w_out, attn_lambda, attn_subln_g,
             mlstm_w_in, mlstm_b_gates, mlstm_norm_g, mlstm_w_out, router_w, expert_w_gate, expert_w_up,
             expert_w_down):
    D, B, T, Tc, E = cfg.D, cfg.B, cfg.T, cfg.Tc, cfg.E
    depth = ada_w.shape[0]
    cos_t, sin_t = rope_tables(cfg)
    cvec = jnp.concatenate([c, c_ctx[None, :], jnp.zeros((8 - B - 1, D), F32)], axis=0)
    ada_b3 = ada_b.reshape(depth, 1, 6 * D)
    norm_g3 = norm_g.reshape(depth * 4, 1, D)
    X = jnp.concatenate([x.reshape(B * T, D), ctx.reshape(B * Tc, D)], axis=0)

    mods = [adaln(cvec, ada_w, ada_b3, i)[:B + 1].reshape((B + 1) * 6, 1, D) for i in range(depth)]
    rw = jnp.pad(router_w, ((0, 0), (0, 0), (0, LANES - E)))
    rw_hi = rw.astype(BF16)
    rw_lo = (rw - rw_hi.astype(F32)).astype(BF16)

    _, h, _ = resid_norm(cfg, X, norm_g3, h_part=(mods[0], 0, 0, 1, 0))
    for i in range(depth):
        last = i == depth - 1
        mod = mods[i]
        j = i // 2
        n_rows = cfg.n_lat if last else cfg.R
        if i % 2 == 0:
            lam_init = 0.8 - 0.6 * math.exp(-0.3 * i)
            w_in = attn_w_in[j]
            w_qk = rope_col_order(cfg, w_in[:, :2 * D]).astype(BF16)
            qk_lat = qk_proj_rope(cfg, h, w_qk, cos_t, sin_t)
            qk_ctx = matmul(h, w_qk, BF16, row0=cfg.n_lat, scale_cols=D,
                            scale=cfg.da_dim ** -0.5 * math.log2(math.e))
            v = matmul(h, w_in[:, 2 * D:].astype(BF16), BF16)
            g2 = attn_subln_g[j].reshape(1, -1)
            o = diff_attention(cfg, qk_lat, qk_ctx, v, attn_lambda[j], g2, lam_init, latent=True)
            if not last:
                o = jnp.concatenate(
                    [o, diff_attention(cfg, qk_lat, qk_ctx, v, attn_lambda[j], g2, lam_init, latent=False)], 0)
            y = matmul(o, attn_w_out[j].astype(BF16), BF16)
        else:
            H = cfg.ml_heads
            w_in = mlstm_w_in[j]
            nmain = w_in.shape[1] - 4 * H
            proj = matmul(h, w_in[:, :nmain].astype(BF16), BF16)
            wg = jnp.pad(w_in[:, nmain:], ((0, 0), (0, LANES - 4 * H))).astype(BF16)
            gates = matmul(h, wg, F32)
            bias = jnp.pad(mlstm_b_gates[j].reshape(1, 4 * H), ((0, 0), (0, LANES - 4 * H)))
            hf = mlstm_scan(cfg, proj, gates, bias, reverse=False)
            hn = mlstm_scan(cfg, proj, gates, bias, reverse=True, hf=hf, norm_g=mlstm_norm_g[j].reshape(1, D))
            y = matmul(hn, mlstm_w_out[j].astype(BF16), BF16, n_rows=n_rows)
        X, h, lg = resid_norm(cfg, X, norm_g3, y_part=(y, mod, i, 2, 1), h_part=(mod, i, 2, 4, 3),
                              router=(rw_hi[i], rw_lo[i]), n_rows=n_rows)
        mo = expert_choice_moe(cfg, h, lg, expert_w_gate, expert_w_up, expert_w_down, i, with_ctx=not last)
        if last:
            X, _, _ = resid_norm(cfg, X, norm_g3, y_part=(mo, mod, i, 5, 3), n_rows=n_rows)
        else:
            X, h, _ = resid_norm(cfg, X, norm_g3, y_part=(mo, mod, i, 5, 3), h_part=(mods[i + 1], i + 1, 0, 1, 0))
    return X.reshape(B, T, D)


def kernel(x, c, ctx, c_ctx, ada_w, ada_b, norm_g, attn_w_in, attn_w_out, attn_lambda, attn_subln_g, mlstm_w_in,
           mlstm_b_gates, mlstm_norm_g, mlstm_w_out, router_w, expert_w_gate, expert_w_up, expert_w_down):
    return _forward(FULL_CFG, x, c, ctx, c_ctx, ada_w, ada_b, norm_g, attn_w_in, attn_w_out, attn_lambda,
                    attn_subln_g, mlstm_w_in, mlstm_b_gates, mlstm_norm_g, mlstm_w_out, router_w, expert_w_gate,
                    expert_w_up, expert_w_down)
```

```python
import functools
import math
from typing import NamedTuple

import jax
import jax.numpy as jnp
from jax import lax
from jax.experimental import pallas as pl
from jax.experimental.pallas import tpu as pltpu

BF16 = jnp.bfloat16
F32 = jnp.float32

EPS = 1e-6
GATE_CAP = 15.0
ROPE_BASE = 10000.0
LANES = 128
MIB = 1024 * 1024
VMEM_LIMIT = 56 * MIB

class Cfg(NamedTuple):
    D: int
    B: int
    T: int
    Tc: int
    grid_w: int
    da_heads: int
    da_dim: int
    ml_heads: int
    E: int
    F: int
    chunk: int
    cap_factor: int

    @property
    def n_lat(self):
        return self.B * self.T

    @property
    def n_ctx(self):
        return self.B * self.Tc

    @property
    def R(self):
        return self.n_lat + self.n_ctx

    @property
    def ml_qk(self):
        return self.D // 2 // self.ml_heads

    @property
    def ml_v(self):
        return self.D // self.ml_heads


FULL_CFG = Cfg(D=4096, B=2, T=8192, Tc=256, grid_w=64, da_heads=16, da_dim=128, ml_heads=8, E=16, F=1024,
               chunk=64, cap_factor=2)


def _cparams(sem, vmem=VMEM_LIMIT):
    return pltpu.CompilerParams(dimension_semantics=sem, vmem_limit_bytes=vmem)


def _tile(n, pref):
    t = min(n, pref)
    while n % t:
        t //= 2
    return t


def _lane_tile(x, reps):
    return x if reps == 1 else jnp.concatenate([x] * reps, axis=1)


def _adaln_kernel(c_ref, w_ref, b_ref, o_ref):
    c = c_ref[...]
    s = (c * jax.nn.sigmoid(c)).astype(BF16)
    o_ref[...] = jnp.dot(s, w_ref[...].astype(BF16), preferred_element_type=F32) + b_ref[...]


def adaln(cvec, ada_w, ada_b3, layer):
    D = cvec.shape[1]
    N = ada_w.shape[2]
    tn = _tile(N, 512)
    return pl.pallas_call(
        _adaln_kernel,
        grid=(N // tn,),
        in_specs=[
            pl.BlockSpec((8, D), lambda j: (0, 0)),
            pl.BlockSpec((None, D, tn), lambda j: (layer, 0, j)),
            pl.BlockSpec((None, 1, tn), lambda j: (layer, 0, j)),
        ],
        out_specs=pl.BlockSpec((8, tn), lambda j: (0, j)),
        out_shape=jax.ShapeDtypeStruct((8, N), F32),
        compiler_params=_cparams(("parallel",)),
        name="adaln",
    )(cvec, ada_w, ada_b3)


def _rms(x, g):
    return x * lax.rsqrt(jnp.mean(x * x, axis=-1, keepdims=True) + EPS) * g


def _resid_norm_kernel(*refs, split_at, has_y, want_h, want_logits):
    it = iter(refs)
    x_ref = next(it)
    xc_ref = next(it) if split_at else None
    if has_y:
        y_ref, gate_ref, gpost_ref = next(it), next(it), next(it)
    if want_h:
        gpre_ref, sc_ref, sh_ref = next(it), next(it), next(it)
    if want_logits:
        rwh_ref, rwl_ref = next(it), next(it)
    if has_y:
        xo_ref = next(it)
    if want_h:
        h_ref = next(it)
    if want_logits:
        lg_ref = next(it)

    x = x_ref[...]
    if split_at:
        x = jnp.where(pl.program_id(0) < split_at, x, xc_ref[...])
    if has_y:
        y = y_ref[...].astype(F32)
        x = x + gate_ref[0] * _rms(y, gpost_ref[0])
        xo_ref[...] = x
    if want_h:
        h = _rms(x, gpre_ref[0]) * (1.0 + sc_ref[0]) + sh_ref[0]
        hb = h.astype(BF16)
        h_ref[...] = hb
        if want_logits:
            hl = (h - hb.astype(F32)).astype(BF16)
            lg_ref[...] = (jnp.dot(hb, rwh_ref[...], preferred_element_type=F32)
                           + jnp.dot(hl, rwh_ref[...], preferred_element_type=F32)
                           + jnp.dot(hb, rwl_ref[...], preferred_element_type=F32))


def resid_norm(cfg, x, norm_g, *, y_part=None, h_part=None, router=None, n_rows=None):
    D = cfg.D
    tm = _tile(cfg.Tc, 128)
    count = n_rows if n_rows is not None else cfg.R
    assert count % tm == 0
    nlat_t = cfg.n_lat // tm
    tpb = cfg.T // tm
    has_y, want_h, want_logits = y_part is not None, h_part is not None, router is not None

    def set_of(i):
        return jnp.where(i < nlat_t, i // tpb, cfg.B)

    row_spec = lambda w: pl.BlockSpec((tm, w), lambda i: (i, 0))
    mod_spec = lambda k: pl.BlockSpec((1, 1, D), lambda i: (set_of(i) * 6 + k, 0, 0))
    g_spec = lambda layer, k: pl.BlockSpec((1, 1, D), lambda i: (layer * 4 + k, 0, 0))

    split = isinstance(x, tuple)
    if split:
        args = list(x)
        in_specs = [pl.BlockSpec((tm, D), lambda i: (jnp.minimum(i, nlat_t - 1), 0)),
                    pl.BlockSpec((tm, D), lambda i: (jnp.maximum(i - nlat_t, 0), 0))]
    else:
        args, in_specs = [x], [row_spec(D)]
    out_shape, out_specs = [], []
    if has_y:
        y, mod, layer, gate_k, gpost_k = y_part
        args += [y, mod, norm_g]
        in_specs += [row_spec(D), mod_spec(gate_k), g_spec(layer, gpost_k)]
    if want_h:
        mod, layer, gpre_k, sc_k, sh_k = h_part
        args += [norm_g, mod, mod]
        in_specs += [g_spec(layer, gpre_k), mod_spec(sc_k), mod_spec(sh_k)]
    if want_logits:
        args += [router[0], router[1]]
        in_specs += [pl.BlockSpec((D, LANES), lambda i: (0, 0))] * 2
    if has_y:
        out_shape.append(jax.ShapeDtypeStruct((count, D), F32))
        out_specs.append(row_spec(D))
    if want_h:
        out_shape.append(jax.ShapeDtypeStruct((count, D), BF16))
        out_specs.append(row_spec(D))
    if want_logits:
        out_shape.append(jax.ShapeDtypeStruct((count, LANES), F32))
        out_specs.append(row_spec(LANES))

    outs = pl.pallas_call(
        functools.partial(_resid_norm_kernel, split_at=nlat_t if split else 0, has_y=has_y, want_h=want_h,
                          want_logits=want_logits),
        grid=(count // tm,),
        in_specs=in_specs,
        out_specs=out_specs,
        out_shape=out_shape,
        compiler_params=_cparams(("parallel",)),
        name="resid_norm",
    )(*args)
    outs = list(outs)
    x_new = outs.pop(0) if has_y else None
    h = outs.pop(0) if want_h else None
    lg = outs.pop(0) if want_logits else None
    return x_new, h, lg


def _mm_kernel(x_ref, w_ref, o_ref, *, scale_tiles, scale):
    acc = jnp.dot(x_ref[...], w_ref[...], preferred_element_type=F32)
    if scale_tiles:
        acc = acc * jnp.where(pl.program_id(1) < scale_tiles, scale, 1.0).astype(F32)
    o_ref[...] = acc.astype(o_ref.dtype)


def matmul(x, w, out_dtype, n_rows=None, row0=0, scale_cols=0, scale=1.0, tm_pref=1024, tn_pref=1024):
    K = x.shape[1]
    M = n_rows if n_rows is not None else x.shape[0] - row0
    N = w.shape[1]
    tm, tn = _tile(math.gcd(M, row0), tm_pref), _tile(math.gcd(N, scale_cols), tn_pref)
    assert row0 % tm == 0 and scale_cols % tn == 0
    r0 = row0 // tm
    return pl.pallas_call(
        functools.partial(_mm_kernel, scale_tiles=scale_cols // tn, scale=scale),
        grid=(M // tm, N // tn),
        in_specs=[pl.BlockSpec((tm, K), lambda i, j: (i + r0, 0)), pl.BlockSpec((K, tn), lambda i, j: (0, j))],
        out_specs=pl.BlockSpec((tm, tn), lambda i, j: (i, j)),
        out_shape=jax.ShapeDtypeStruct((M, N), out_dtype),
        compiler_params=_cparams(("parallel", "arbitrary")),
        name="matmul",
    )(x, w)


def _qk_rope_kernel(x_ref, w_ref, cos_ref, sin_ref, o_ref, *, q_tiles, qscale):
    acc = jnp.dot(x_ref[...], w_ref[...], preferred_element_type=F32)
    acc = acc * jnp.where(pl.program_id(1) < q_tiles, qscale, 1.0).astype(F32)
    cos, sin = cos_ref[...], sin_ref[...]
    for g in range(acc.shape[1] // LANES):
        xg = acc[:, g * LANES:(g + 1) * LANES]
        o_ref[:, g * LANES:(g + 1) * LANES] = (xg * cos + pltpu.roll(xg, LANES // 2, 1) * sin).astype(o_ref.dtype)


def qk_proj_rope(cfg, h, w_qk, cos_t, sin_t):
    D = cfg.D
    N = w_qk.shape[1]
    tm = _tile(cfg.T, 512)
    tn = _tile(D, 512)
    tpb = cfg.T // tm
    qscale = cfg.da_dim ** -0.5 * math.log2(math.e)
    tbl = pl.BlockSpec((tm, LANES), lambda i, j: (i % tpb, 0))
    return pl.pallas_call(
        functools.partial(_qk_rope_kernel, q_tiles=D // tn, qscale=qscale),
        grid=(cfg.n_lat // tm, N // tn),
        in_specs=[pl.BlockSpec((tm, D), lambda i, j: (i, 0)), pl.BlockSpec((D, tn), lambda i, j: (0, j)), tbl, tbl],
        out_specs=pl.BlockSpec((tm, tn), lambda i, j: (i, j)),
        out_shape=jax.ShapeDtypeStruct((cfg.n_lat, N), BF16),
        compiler_params=_cparams(("parallel", "arbitrary")),
        name="qk_proj_rope",
    )(h, w_qk, cos_t, sin_t)


def rope_col_order(cfg, w):
    K, N = w.shape
    q = cfg.da_dim // 4
    return w.reshape(K, N // cfg.da_dim, 2, 2, q).transpose(0, 1, 3, 2, 4).reshape(K, N)


def rope_tables(cfg):
    half = cfg.da_dim // 2
    t = jnp.arange(cfg.T)
    row = (t // cfg.grid_w).astype(F32)
    col = (t % cfg.grid_w).astype(F32)
    inv = ROPE_BASE ** (-jnp.arange(0, half, 2, dtype=F32) / half)
    ar, ac = row[:, None] * inv, col[:, None] * inv
    cos_t = jnp.concatenate([jnp.cos(ar), jnp.cos(ac), jnp.cos(ar), jnp.cos(ac)], axis=1)
    sin_t = jnp.concatenate([-jnp.sin(ar), -jnp.sin(ac), jnp.sin(ar), jnp.sin(ac)], axis=1)
    return cos_t, sin_t


def _attn_kernel(*refs, tk, n_lat, lam_init, dd):
    if n_lat:
        (lam_ref, g_ref, q_ref, kc_ref, vc_ref, kl_ref, vl_ref, o_ref, acc_ref, m_ref, l_ref, sa_ref,
         sb_ref) = refs
    else:
        lam_ref, g_ref, q_ref, kc_ref, vc_ref, o_ref, acc_ref, m_ref, l_ref = refs
    d = dd // 2
    q = q_ref[...]

    m_ref[...] = jnp.full(m_ref.shape, -jnp.inf, F32)
    l_ref[...] = jnp.zeros(l_ref.shape, F32)
    acc_ref[...] = jnp.zeros(acc_ref.shape, F32)

    def score(k, p):
        return lax.dot_general(q[:, p * d:(p + 1) * d], k[:, p * d:(p + 1) * d], (((1,), (1,)), ((), ())),
                               preferred_element_type=F32)

    def softmax_pv(s_of, v):
        n = v.shape[0]
        nh = n // 2 if (n // 2) % LANES == 0 else n
        for p in range(2):
            m_prev = m_ref[p]
            m_next = jnp.maximum(m_prev, jnp.max(s_of(p, slice(0, n)), axis=1)[:, None])
            alpha = jnp.exp2(m_prev - m_next)
            m_rep = _lane_tile(m_next, nh // LANES)
            l_new = alpha * l_ref[p]
            acc = acc_ref[p] * _lane_tile(alpha, dd // LANES)
            for c0 in range(0, n, nh):
                pr = jnp.exp2(s_of(p, slice(c0, c0 + nh)) - m_rep)
                l_new = l_new + jnp.sum(pr, axis=1)[:, None]
                acc = acc + jnp.dot(pr.astype(BF16), v[c0:c0 + nh, :], preferred_element_type=F32)
            l_ref[p] = l_new
            m_ref[p] = m_next
            acc_ref[p] = acc

    kc = kc_ref[...]
    sc = [score(kc, p) for p in range(2)]
    if not n_lat:
        softmax_pv(lambda p, cols: sc[p][:, cols], vc_ref[...])
    else:
        def chunk(ref, j):
            return ref[pl.ds(pl.multiple_of(j * tk, tk), tk), :]

        def scores_to(s_ref, j):
            k = chunk(kl_ref, j)
            for p in range(2):
                s_ref[p] = score(k, p)

        scores_to(sa_ref, 0)
        softmax_pv(lambda p, cols: sc[p][:, cols], vc_ref[...])

        def body(jj, carry):
            j0 = 2 * jj
            scores_to(sb_ref, j0 + 1)
            softmax_pv(lambda p, cols: sa_ref[p, :, cols],chunk(vl_ref, j0))
            scores_to(sa_ref, j0 + 2)
            softmax_pv(lambda p, cols: sb_ref[p, :, cols],chunk(vl_ref, j0 + 1))
            return carry

        lax.fori_loop(0, n_lat // 2 - 1, body, 0)
        scores_to(sb_ref, n_lat - 1)
        softmax_pv(lambda p, cols: sa_ref[p, :, cols],chunk(vl_ref, n_lat - 2))
        softmax_pv(lambda p, cols: sb_ref[p, :, cols],chunk(vl_ref, n_lat - 1))

    lv = lam_ref[...]
    lam = (jnp.exp(jnp.sum(lv[0:1] * lv[1:2], axis=1, keepdims=True))
           - jnp.exp(jnp.sum(lv[2:3] * lv[3:4], axis=1, keepdims=True)) + lam_init)
    o1 = acc_ref[0] / _lane_tile(l_ref[0], dd // LANES)
    o2 = acc_ref[1] / _lane_tile(l_ref[1], dd // LANES)
    o = o1 - lam * o2
    o = _rms(o, g_ref[...]) * (1.0 - lam_init)
    o_ref[...] = o.astype(o_ref.dtype)


def diff_attention(cfg, qk_lat, qk_ctx, v, lam_vecs, subln_g, lam_init, *, latent):
    D, B, T, Tc, H = cfg.D, cfg.B, cfg.T, cfg.Tc, cfg.da_heads
    dd = 2 * cfg.da_dim
    assert T % Tc == 0 and dd % LANES == 0 and Tc % LANES == 0
    ctx_blk0 = cfg.n_lat // Tc
    small = lambda shape: pl.BlockSpec(shape, lambda b, h, i: (0, 0))
    kv_ctx = [pl.BlockSpec((Tc, dd), lambda b, h, i: (b, H + h)),
              pl.BlockSpec((Tc, dd), lambda b, h, i: (ctx_blk0 + b, h))]
    if latent:
        tq, tk = _tile(T, 512), _tile(T // 2, 1024)
        nq, n_lat = T // tq, T // tk
        assert n_lat % 2 == 0
        in_specs = [small((4, cfg.da_dim)), small((1, dd)),
                    pl.BlockSpec((tq, dd), lambda b, h, i: (b * nq + i, h))] + kv_ctx + [
            pl.BlockSpec((T, dd), lambda b, h, i: (b, H + h)),
            pl.BlockSpec((T, dd), lambda b, h, i: (b, h))]
        args = [lam_vecs, subln_g, qk_lat, qk_ctx, v, qk_lat, v]
        extra = [pltpu.VMEM((2, tq, tk), F32), pltpu.VMEM((2, tq, tk), F32)]
    else:
        tq, tk, nq, n_lat = Tc, 0, 1, 0
        in_specs = [small((4, cfg.da_dim)), small((1, dd)),
                    pl.BlockSpec((tq, dd), lambda b, h, i: (b, h))] + kv_ctx
        args = [lam_vecs, subln_g, qk_ctx, qk_ctx, v]
        extra = []
    scratch = [pltpu.VMEM((2, tq, dd), F32), pltpu.VMEM((2, tq, LANES), F32), pltpu.VMEM((2, tq, LANES), F32)]
    n_out = B * (T if latent else Tc)
    return pl.pallas_call(
        functools.partial(_attn_kernel, tk=tk, n_lat=n_lat, lam_init=lam_init, dd=dd),
        grid=(B, H, nq),
        in_specs=in_specs,
        out_specs=pl.BlockSpec((tq, dd), lambda b, h, i: (b * nq + i, h)),
        out_shape=jax.ShapeDtypeStruct((n_out, D), BF16),
        scratch_shapes=scratch + extra,
        compiler_params=_cparams(("parallel", "parallel", "arbitrary")),
        name="diff_attn_lat" if latent else "diff_attn_ctx",
    )(*args)


def _time_cumsum(x, reverse):
    L = x.shape[0]
    row = lax.broadcasted_iota(jnp.int32, x.shape, 0)
    sh = 1
    while sh < L:
        if reverse:
            x = x + jnp.where(row < L - sh, pltpu.roll(x, L - sh, 0), 0.0)
        else:
            x = x + jnp.where(row >= sh, pltpu.roll(x, sh, 0), 0.0)
        sh *= 2
    return x


def _mlstm_kernel(*refs, H, dqk, dv, L, reverse, finish, gate_off):
    if finish:
        (q_ref, k_ref, v_ref, gt_ref, bias_ref, hf_ref, og_ref, ng_ref, o_ref, c_ref, n_ref, m_ref) = refs
    else:
        (q_ref, k_ref, v_ref, gt_ref, bias_ref, o_ref, c_ref, n_ref, m_ref) = refs

    @pl.when(pl.program_id(1) == 0)
    def _():
        c_ref[...] = jnp.zeros(c_ref.shape, F32)
        n_ref[...] = jnp.zeros(n_ref.shape, F32)
        m_ref[...] = jnp.zeros(m_ref.shape, F32)

    gates = GATE_CAP * jnp.tanh((gt_ref[...] + bias_ref[...]) * (1.0 / GATE_CAP))
    ig = pltpu.roll(gates, LANES - gate_off, 1) if gate_off else gates
    fg = pltpu.roll(gates, LANES - gate_off - H, 1)
    lf = jnp.minimum(fg, 0.0) - jnp.log1p(jnp.exp(-jnp.abs(fg)))
    bc = _time_cumsum(lf, reverse)
    m_prev = m_ref[0:1, :]
    a_all = bc + m_prev
    b_last = bc[0:1, :] if reverse else bc[L - 1:L, :]
    g_all = b_last - bc + ig
    m_new = jnp.maximum(b_last + m_prev, jnp.max(g_all, axis=0, keepdims=True))
    decay = jnp.exp(b_last + m_prev - m_new)
    eg = jnp.exp(g_all - m_new)
    xt = (ig - bc).T
    m_ref[...] = jnp.broadcast_to(m_new, m_ref.shape)

    li = lax.broadcasted_iota(jnp.int32, (L, L), 0)
    si = lax.broadcasted_iota(jnp.int32, (L, L), 1)
    tri = (si >= li) if reverse else (si <= li)
    qscale = dqk ** -0.5

    for h in range(H):
        q = (q_ref[:, h * dqk:(h + 1) * dqk].astype(F32) * qscale).astype(BF16)
        k = k_ref[:, h * dqk:(h + 1) * dqk]
        v = v_ref[:, h * dv:(h + 1) * dv]
        ct = c_ref[h]
        nv = n_ref[h]
        a_h = a_all[:, h:h + 1]
        dmat = jnp.where(tri, bc[:, h:h + 1] + xt[h:h + 1, :], -jnp.inf)
        m_t = jnp.maximum(a_h, jnp.max(dmat, axis=1, keepdims=True))
        qk = lax.dot_general(q, k, (((1,), (1,)), ((), ())), preferred_element_type=F32)
        w = jnp.exp(dmat - m_t) * qk
        inter = jnp.exp(a_h - m_t)
        num = inter * jnp.dot(q, ct.astype(BF16), preferred_element_type=F32) + jnp.dot(
            w.astype(BF16), v, preferred_element_type=F32)
        qn = jnp.sum(q.astype(F32) * nv, axis=1, keepdims=True)
        den = inter * qn + jnp.sum(w, axis=1, keepdims=True)
        hout = num / jnp.maximum(jnp.abs(den), jnp.exp(-m_t))

        dec_h = decay[:, h:h + 1]
        wk = eg[:, h:h + 1] * k.astype(F32)
        c_ref[h] = dec_h * ct + lax.dot_general(wk.astype(BF16), v, (((0,), (0,)), ((), ())),
                                                preferred_element_type=F32)
        n_ref[h] = dec_h * nv + jnp.sum(wk, axis=0, keepdims=True)

        if finish:
            hs = hout + hf_ref[:, h * dv:(h + 1) * dv]
            hn = _rms(hs, ng_ref[:, h * dv:(h + 1) * dv]).astype(BF16)
            og = og_ref[:, h * dv:(h + 1) * dv]
            o_ref[:, h * dv:(h + 1) * dv] = (hn * jax.nn.sigmoid(og)).astype(o_ref.dtype)
        else:
            o_ref[:, h * dv:(h + 1) * dv] = hout.astype(o_ref.dtype)


def mlstm_scan(cfg, proj, gates, bias, *, reverse, hf=None, norm_g=None):
    D, B, T, Tc, H, L = cfg.D, cfg.B, cfg.T, cfg.Tc, cfg.ml_heads, cfg.chunk
    dqk, dv = cfg.ml_qk, cfg.ml_v
    ncc, ncl = Tc // L, T // L
    nc = ncc + ncl
    finish = hf is not None
    qw = H * dqk
    assert D % qw == 0 and (H * dv) == D

    def rb(b, c):
        if reverse:
            return jnp.where(c < ncc, cfg.n_lat // L + b * ncc + (ncc - 1 - c), b * ncl + (nc - 1 - c))
        return jnp.where(c < ncc, cfg.n_lat // L + b * ncc + c, b * ncl + (c - ncc))

    in_specs = [
        pl.BlockSpec((L, qw), lambda b, c: (rb(b, c), 0)),
        pl.BlockSpec((L, qw), lambda b, c: (rb(b, c), 1)),
        pl.BlockSpec((L, D), lambda b, c: (rb(b, c), 2 * qw // D)),
        pl.BlockSpec((L, LANES), lambda b, c: (rb(b, c), 0)),
        pl.BlockSpec((1, LANES), lambda b, c: (0, 0)),
    ]
    args = [proj, proj, proj, gates, bias]
    if finish:
        in_specs += [
            pl.BlockSpec((L, D), lambda b, c: (rb(b, c), 0)),
            pl.BlockSpec((L, D), lambda b, c: (rb(b, c), 2 * qw // D + 1)),
            pl.BlockSpec((1, D), lambda b, c: (0, 0)),
        ]
        args += [hf, proj, norm_g]
    return pl.pallas_call(
        functools.partial(_mlstm_kernel, H=H, dqk=dqk, dv=dv, L=L, reverse=reverse, finish=finish,
                          gate_off=2 * H if reverse else 0),
        grid=(B, nc),
        in_specs=in_specs,
        out_specs=pl.BlockSpec((L, D), lambda b, c: (rb(b, c), 0)),
        out_shape=jax.ShapeDtypeStruct((cfg.R, D), BF16 if finish else F32),
        scratch_shapes=[pltpu.VMEM((H, dqk, dv), F32), pltpu.VMEM((H, 1, dqk), F32), pltpu.VMEM((8, LANES), F32)],
        compiler_params=_cparams(("parallel", "arbitrary")),
        name="mlstm_bwd" if reverse else "mlstm_fwd",
    )(*args)


def _moe_up_kernel(x_ref, wg_ref, wu_ref, o_ref, wgb_ref, wub_ref):
    @pl.when(pl.program_id(2) == 0)
    def _():
        wgb_ref[...] = wg_ref[...].astype(BF16)
        wub_ref[...] = wu_ref[...].astype(BF16)

    x = x_ref[...]
    a = jnp.dot(x, wgb_ref[...], preferred_element_type=F32)
    u = jnp.dot(x, wub_ref[...], preferred_element_type=F32)
    o_ref[...] = (a * jax.nn.sigmoid(a) * u).astype(o_ref.dtype)


def _row_tile(n, cap):
    best = n
    for t in range(16, min(n, cap) + 1, 16):
        if n % t == 0:
            best = t
    return best


def moe_up(cfg, xe, w_gate, w_up, layer):
    E, P, D = xe.shape
    F = cfg.F
    tp = _row_tile(P, 1056)
    fk = _tile(F, 256)
    w_spec = pl.BlockSpec((None, None, D, fk), lambda e, f, p: (layer, e, 0, f))
    return pl.pallas_call(
        _moe_up_kernel,
        grid=(E, F // fk, P // tp),
        in_specs=[pl.BlockSpec((None, tp, D), lambda e, f, p: (e, p, 0)), w_spec, w_spec],
        out_specs=pl.BlockSpec((None, tp, fk), lambda e, f, p: (e, p, f)),
        out_shape=jax.ShapeDtypeStruct((E, P, F), BF16),
        scratch_shapes=[pltpu.VMEM((D, fk), BF16), pltpu.VMEM((D, fk), BF16)],
        compiler_params=_cparams(("parallel", "parallel", "arbitrary")),
        name="moe_up",
    )(xe, w_gate, w_up)


def _moe_down_kernel(rows_ref, hid_ref, g_ref, wd_ref, acc_in_ref, acc_ref, wdb_ref, buf_ref, sem_in, sem_out, *,
                     tp, P, nblk):
    del acc_in_ref
    e, p = pl.program_id(0), pl.program_id(1)
    step = e * pl.num_programs(1) + p
    last = pl.num_programs(0) * pl.num_programs(1) - 1
    slot = lax.rem(step, 2)
    base = e * P + p * tp
    blk = tp // nblk

    def tile_copy(s, sem):
        return pltpu.make_async_copy(acc_ref.at[pl.ds(0, tp)], buf_ref.at[s], sem)

    @pl.when(p == 0)
    def _():
        wdb_ref[...] = wd_ref[...].astype(BF16)

        @pl.when(step > 0)
        def _():
            tile_copy(1 - slot, sem_out.at[1 - slot]).wait()

    ys = []
    for i in range(nblk):
        r0 = i * blk
        ys.append(jnp.dot(hid_ref[r0:r0 + blk, :], wdb_ref[...], preferred_element_type=F32)
                  * g_ref[r0:r0 + blk, :])
        for r in range(r0, r0 + blk):
            pltpu.make_async_copy(acc_ref.at[pl.ds(rows_ref[base + r], 1)], buf_ref.at[slot, pl.ds(r, 1)],
                                  sem_in).start()
    tile_copy(slot, sem_in).wait()
    for i in range(nblk):
        r0 = i * blk
        buf_ref[slot, r0:r0 + blk, :] += ys[i]
        for r in range(r0, r0 + blk):
            pltpu.make_async_copy(buf_ref.at[slot, pl.ds(r, 1)], acc_ref.at[pl.ds(rows_ref[base + r], 1)],
                                  sem_out.at[slot]).start()

    @pl.when(jnp.logical_and(p != 0, step > 0))
    def _():
        tile_copy(1 - slot, sem_out.at[1 - slot]).wait()

    @pl.when(step == last)
    def _():
        tile_copy(slot, sem_out.at[slot]).wait()


def moe_down_scatter(cfg, rows, hid, ge, w_down, layer, n_rows):
    E, P, F = hid.shape
    D = cfg.D
    tp = _row_tile(P, 352)
    grid_spec = pltpu.PrefetchScalarGridSpec(
        num_scalar_prefetch=1,
        grid=(E, P // tp),
        in_specs=[
            pl.BlockSpec((None, tp, F), lambda e, p, rows: (e, p, 0)),
            pl.BlockSpec((None, tp, 1), lambda e, p, rows: (e, p, 0)),
            pl.BlockSpec((None, None, F, D), lambda e, p, rows: (layer, e, 0, 0), pipeline_mode=pl.Buffered(1)),
            pl.BlockSpec(memory_space=pl.ANY),
        ],
        out_specs=pl.BlockSpec(memory_space=pl.ANY),
        scratch_shapes=[pltpu.VMEM((F, D), BF16), pltpu.VMEM((2, tp, D), F32), pltpu.SemaphoreType.DMA(()),
                        pltpu.SemaphoreType.DMA((2,))],
    )
    nblk = 2 if tp % 32 == 0 else 1
    return pl.pallas_call(
        functools.partial(_moe_down_kernel, tp=tp, P=P, nblk=nblk),
        grid_spec=grid_spec,
        out_shape=jax.ShapeDtypeStruct((n_rows, D), F32),
        input_output_aliases={4: 0},
        compiler_params=_cparams(("arbitrary", "arbitrary")),
        name="moe_down_scatter",
    )(rows.reshape(-1), hid, ge, w_down, jnp.zeros((n_rows, D), F32))


def expert_choice_moe(cfg, h, logits, w_gate, w_up, w_down, layer, with_ctx):
    B, T, Tc, E = cfg.B, cfg.T, cfg.Tc, cfg.E
    n_rows = cfg.R if with_ctx else cfg.n_lat
    aff = jax.nn.softmax(logits[:n_rows, :E], axis=-1)
    sets = [(b * T, T) for b in range(B)]
    if with_ctx:
        sets += [(cfg.n_lat + b * Tc, Tc) for b in range(B)]
    rows, gates = [], []
    for start, n in sets:
        cap = max(1, (cfg.cap_factor * n) // E)
        g, idx = lax.top_k(aff[start:start + n].T, cap)
        rows.append(idx + start)
        gates.append(g)
    rows = jnp.concatenate(rows, axis=1)
    gates = jnp.concatenate(gates, axis=1)
    xe = h[rows]
    hid = moe_up(cfg, xe, w_gate, w_up, layer)
    return moe_down_scatter(cfg, rows, hid, gates[..., None], w_down, layer, n_rows)


def _forward(cfg, x, c, ctx, c_ctx, ada_w, ada_b, norm_g, attn_w_in, attn_w_out, attn_lambda, attn_subln_g,
             mlstm_w_in, mlstm_b_gates, mlstm_norm_g, mlstm_w_out, router_w, expert_w_gate, expert_w_up,
             expert_w_down):
    D, B, T, Tc, E = cfg.D, cfg.B, cfg.T, cfg.Tc, cfg.E
    depth = ada_w.shape[0]
    cos_t, sin_t = rope_tables(cfg)
    cvec = jnp.concatenate([c, c_ctx[None, :], jnp.zeros((8 - B - 1, D), F32)], axis=0)
    ada_b3 = ada_b.reshape(depth, 1, 6 * D)
    norm_g3 = norm_g.reshape(depth * 4, 1, D)
    X = (x.reshape(B * T, D), ctx.reshape(B * Tc, D))

    mods = [adaln(cvec, ada_w, ada_b3, i)[:B + 1].reshape((B + 1) * 6, 1, D) for i in range(depth)]
    rw = jnp.pad(router_w, ((0, 0), (0, 0), (0, LANES - E)))
    rw_hi = rw.astype(BF16)
    rw_lo = (rw - rw_hi.astype(F32)).astype(BF16)

    _, h, _ = resid_norm(cfg, X, norm_g3, h_part=(mods[0], 0, 0, 1, 0))
    for i in range(depth):
        last = i == depth - 1
        mod = mods[i]
        j = i // 2
        n_rows = cfg.n_lat if last else cfg.R
        if i % 2 == 0:
            lam_init = 0.8 - 0.6 * math.exp(-0.3 * i)
            w_in = attn_w_in[j]
            w_qk = rope_col_order(cfg, w_in[:, :2 * D]).astype(BF16)
            qk_lat = qk_proj_rope(cfg, h, w_qk, cos_t, sin_t)
            qk_ctx = matmul(h, w_qk, BF16, row0=cfg.n_lat, scale_cols=D,
                            scale=cfg.da_dim ** -0.5 * math.log2(math.e))
            v = matmul(h, w_in[:, 2 * D:].astype(BF16), BF16)
            g2 = attn_subln_g[j].reshape(1, -1)
            o = diff_attention(cfg, qk_lat, qk_ctx, v, attn_lambda[j], g2, lam_init, latent=True)
            if not last:
                o = jnp.concatenate(
                    [o, diff_attention(cfg, qk_lat, qk_ctx, v, attn_lambda[j], g2, lam_init, latent=False)], 0)
            y = matmul(o, attn_w_out[j].astype(BF16), BF16)
        else:
            H = cfg.ml_heads
            w_in = mlstm_w_in[j]
            nmain = w_in.shape[1] - 4 * H
            proj = matmul(h, w_in[:, :nmain].astype(BF16), BF16)
            wg = jnp.pad(w_in[:, nmain:], ((0, 0), (0, LANES - 4 * H))).astype(BF16)
            gates = matmul(h, wg, F32)
            bias = jnp.pad(mlstm_b_gates[j].reshape(1, 4 * H), ((0, 0), (0, LANES - 4 * H)))
            hf = mlstm_scan(cfg, proj, gates, bias, reverse=False)
            hn = mlstm_scan(cfg, proj, gates, bias, reverse=True, hf=hf, norm_g=mlstm_norm_g[j].reshape(1, D))
            y = matmul(hn, mlstm_w_out[j].astype(BF16), BF16, n_rows=n_rows)
        X, h, lg = resid_norm(cfg, X, norm_g3, y_part=(y, mod, i, 2, 1), h_part=(mod, i, 2, 4, 3),
                              router=(rw_hi[i], rw_lo[i]), n_rows=n_rows)
        mo = expert_choice_moe(cfg, h, lg, expert_w_gate, expert_w_up, expert_w_down, i, with_ctx=not last)
        if last:
            X, _, _ = resid_norm(cfg, X, norm_g3, y_part=(mo, mod, i, 5, 3), n_rows=n_rows)
        else:
            X, h, _ = resid_norm(cfg, X, norm_g3, y_part=(mo, mod, i, 5, 3), h_part=(mods[i + 1], i + 1, 0, 1, 0))
    return X.reshape(B, T, D)


def kernel(x, c, ctx, c_ctx, ada_w, ada_b, norm_g, attn_w_in, attn_w_out, attn_lambda, attn_subln_g, mlstm_w_in,
           mlstm_b_gates, mlstm_norm_g, mlstm_w_out, router_w, expert_w_gate, expert_w_up, expert_w_down):
    return _forward(FULL_CFG, x, c, ctx, c_ctx, ada_w, ada_b, norm_g, attn_w_in, attn_w_out, attn_lambda,
                    attn_subln_g, mlstm_w_in, mlstm_b_gates, mlstm_norm_g, mlstm_w_out, router_w, expert_w_gate,
                    expert_w_up, expert_w_down)
```

```python
import functools
import math
from typing import NamedTuple

import jax
import jax.numpy as jnp
from jax import lax
from jax.experimental import pallas as pl
from jax.experimental.pallas import tpu as pltpu

BF16 = jnp.bfloat16
F32 = jnp.float32

EPS = 1e-6
GATE_CAP = 15.0
ROPE_BASE = 10000.0
LANES = 128
MIB = 1024 * 1024
VMEM_LIMIT = 56 * MIB

class Cfg(NamedTuple):
    D: int
    B: int
    T: int
    Tc: int
    grid_w: int
    da_heads: int
    da_dim: int
    ml_heads: int
    E: int
    F: int
    chunk: int
    cap_factor: int

    @property
    def n_lat(self):
        return self.B * self.T

    @property
    def n_ctx(self):
        return self.B * self.Tc

    @property
    def R(self):
        return self.n_lat + self.n_ctx

    @property
    def ml_qk(self):
        return self.D // 2 // self.ml_heads

    @property
    def ml_v(self):
        return self.D // self.ml_heads


FULL_CFG = Cfg(D=4096, B=2, T=8192, Tc=256, grid_w=64, da_heads=16, da_dim=128, ml_heads=8, E=16, F=1024,
               chunk=256, cap_factor=2)


def _cparams(sem, vmem=VMEM_LIMIT):
    return pltpu.CompilerParams(dimension_semantics=sem, vmem_limit_bytes=vmem)


def _tile(n, pref):
    t = min(n, pref)
    while n % t:
        t //= 2
    return t


def _lane_tile(x, reps):
    return x if reps == 1 else jnp.concatenate([x] * reps, axis=1)


def _adaln_kernel(c_ref, w_ref, b_ref, o_ref):
    c = c_ref[...]
    s = (c * jax.nn.sigmoid(c)).astype(BF16)
    o_ref[...] = jnp.dot(s, w_ref[...].astype(BF16), preferred_element_type=F32) + b_ref[...]


def adaln(cvec, ada_w, ada_b3, layer):
    D = cvec.shape[1]
    N = ada_w.shape[2]
    tn = _tile(N, 512)
    return pl.pallas_call(
        _adaln_kernel,
        grid=(N // tn,),
        in_specs=[
            pl.BlockSpec((8, D), lambda j: (0, 0)),
            pl.BlockSpec((None, D, tn), lambda j: (layer, 0, j)),
            pl.BlockSpec((None, 1, tn), lambda j: (layer, 0, j)),
        ],
        out_specs=pl.BlockSpec((8, tn), lambda j: (0, j)),
        out_shape=jax.ShapeDtypeStruct((8, N), F32),
        compiler_params=_cparams(("parallel",)),
        name="adaln",
    )(cvec, ada_w, ada_b3)


def _rms(x, g):
    return x * lax.rsqrt(jnp.mean(x * x, axis=-1, keepdims=True) + EPS) * g


def _resid_norm_kernel(*refs, split_at, has_y, want_h, want_logits):
    it = iter(refs)
    x_ref = next(it)
    xc_ref = next(it) if split_at else None
    if has_y:
        y_ref, gate_ref, gpost_ref = next(it), next(it), next(it)
    if want_h:
        gpre_ref, sc_ref, sh_ref = next(it), next(it), next(it)
    if want_logits:
        rwh_ref, rwl_ref = next(it), next(it)
    if has_y:
        xo_ref = next(it)
    if want_h:
        h_ref = next(it)
    if want_logits:
        lg_ref = next(it)

    x = x_ref[...]
    if split_at:
        x = jnp.where(pl.program_id(0) < split_at, x, xc_ref[...])
    if has_y:
        y = y_ref[...].astype(F32)
        x = x + gate_ref[0] * _rms(y, gpost_ref[0])
        xo_ref[...] = x
    if want_h:
        h = _rms(x, gpre_ref[0]) * (1.0 + sc_ref[0]) + sh_ref[0]
        hb = h.astype(BF16)
        h_ref[...] = hb
        if want_logits:
            hl = (h - hb.astype(F32)).astype(BF16)
            lg_ref[...] = (jnp.dot(hb, rwh_ref[...], preferred_element_type=F32)
                           + jnp.dot(hl, rwh_ref[...], preferred_element_type=F32)
                           + jnp.dot(hb, rwl_ref[...], preferred_element_type=F32))


def resid_norm(cfg, x, norm_g, *, y_part=None, h_part=None, router=None, n_rows=None):
    D = cfg.D
    tm = _tile(cfg.Tc, 128)
    count = n_rows if n_rows is not None else cfg.R
    assert count % tm == 0
    nlat_t = cfg.n_lat // tm
    tpb = cfg.T // tm
    has_y, want_h, want_logits = y_part is not None, h_part is not None, router is not None

    def set_of(i):
        return jnp.where(i < nlat_t, i // tpb, cfg.B)

    row_spec = lambda w: pl.BlockSpec((tm, w), lambda i: (i, 0))
    mod_spec = lambda k: pl.BlockSpec((1, 1, D), lambda i: (set_of(i) * 6 + k, 0, 0))
    g_spec = lambda layer, k: pl.BlockSpec((1, 1, D), lambda i: (layer * 4 + k, 0, 0))

    split = isinstance(x, tuple)
    if split:
        args = list(x)
        in_specs = [pl.BlockSpec((tm, D), lambda i: (jnp.minimum(i, nlat_t - 1), 0)),
                    pl.BlockSpec((tm, D), lambda i: (jnp.maximum(i - nlat_t, 0), 0))]
    else:
        args, in_specs = [x], [row_spec(D)]
    out_shape, out_specs = [], []
    if has_y:
        y, mod, layer, gate_k, gpost_k = y_part
        args += [y, mod, norm_g]
        in_specs += [row_spec(D), mod_spec(gate_k), g_spec(layer, gpost_k)]
    if want_h:
        mod, layer, gpre_k, sc_k, sh_k = h_part
        args += [norm_g, mod, mod]
        in_specs += [g_spec(layer, gpre_k), mod_spec(sc_k), mod_spec(sh_k)]
    if want_logits:
        args += [router[0], router[1]]
        in_specs += [pl.BlockSpec((D, LANES), lambda i: (0, 0))] * 2
    if has_y:
        out_shape.append(jax.ShapeDtypeStruct((count, D), F32))
        out_specs.append(row_spec(D))
    if want_h:
        out_shape.append(jax.ShapeDtypeStruct((count, D), BF16))
        out_specs.append(row_spec(D))
    if want_logits:
        out_shape.append(jax.ShapeDtypeStruct((count, LANES), F32))
        out_specs.append(row_spec(LANES))

    outs = pl.pallas_call(
        functools.partial(_resid_norm_kernel, split_at=nlat_t if split else 0, has_y=has_y, want_h=want_h,
                          want_logits=want_logits),
        grid=(count // tm,),
        in_specs=in_specs,
        out_specs=out_specs,
        out_shape=out_shape,
        compiler_params=_cparams(("parallel",)),
        name="resid_norm",
    )(*args)
    outs = list(outs)
    x_new = outs.pop(0) if has_y else None
    h = outs.pop(0) if want_h else None
    lg = outs.pop(0) if want_logits else None
    return x_new, h, lg


def _mm_kernel(x_ref, w_ref, o_ref, *, scale_tiles, scale):
    acc = jnp.dot(x_ref[...], w_ref[...], preferred_element_type=F32)
    if scale_tiles:
        acc = acc * jnp.where(pl.program_id(1) < scale_tiles, scale, 1.0).astype(F32)
    o_ref[...] = acc.astype(o_ref.dtype)


def matmul(x, w, out_dtype, n_rows=None, row0=0, scale_cols=0, scale=1.0, tm_pref=1024, tn_pref=1024):
    K = x.shape[1]
    M = n_rows if n_rows is not None else x.shape[0] - row0
    N = w.shape[1]
    tm, tn = _tile(math.gcd(M, row0), tm_pref), _tile(math.gcd(N, scale_cols), tn_pref)
    assert row0 % tm == 0 and scale_cols % tn == 0
    r0 = row0 // tm
    return pl.pallas_call(
        functools.partial(_mm_kernel, scale_tiles=scale_cols // tn, scale=scale),
        grid=(M // tm, N // tn),
        in_specs=[pl.BlockSpec((tm, K), lambda i, j: (i + r0, 0)), pl.BlockSpec((K, tn), lambda i, j: (0, j))],
        out_specs=pl.BlockSpec((tm, tn), lambda i, j: (i, j)),
        out_shape=jax.ShapeDtypeStruct((M, N), out_dtype),
        compiler_params=_cparams(("parallel", "arbitrary")),
        name="matmul",
    )(x, w)


def _mm2_kernel(xa_ref, xb_ref, w_ref, o_ref, *, na_tiles):
    x = jnp.where(pl.program_id(0) < na_tiles, xa_ref[...], xb_ref[...])
    o_ref[...] = jnp.dot(x, w_ref[...], preferred_element_type=F32).astype(o_ref.dtype)


def matmul_stacked(xa, xb, w, out_dtype, tm_pref=512, tn_pref=1024):
    (Ma, K), Mb, N = xa.shape, xb.shape[0], w.shape[1]
    tm, tn = _tile(math.gcd(Ma, Mb), tm_pref), _tile(N, tn_pref)
    na = Ma // tm
    return pl.pallas_call(
        functools.partial(_mm2_kernel, na_tiles=na),
        grid=((Ma + Mb) // tm, N // tn),
        in_specs=[pl.BlockSpec((tm, K), lambda i, j: (jnp.minimum(i, na - 1), 0)),
                  pl.BlockSpec((tm, K), lambda i, j: (jnp.maximum(i - na, 0), 0)),
                  pl.BlockSpec((K, tn), lambda i, j: (0, j))],
        out_specs=pl.BlockSpec((tm, tn), lambda i, j: (i, j)),
        out_shape=jax.ShapeDtypeStruct((Ma + Mb, N), out_dtype),
        compiler_params=_cparams(("parallel", "arbitrary")),
        name="matmul_stacked",
    )(xa, xb, w)


def _qk_rope_kernel(x_ref, w_ref, cos_ref, sin_ref, o_ref, *, q_tiles, qscale):
    acc = jnp.dot(x_ref[...], w_ref[...], preferred_element_type=F32)
    acc = acc * jnp.where(pl.program_id(1) < q_tiles, qscale, 1.0).astype(F32)
    cos, sin = cos_ref[...], sin_ref[...]
    for g in range(acc.shape[1] // LANES):
        xg = acc[:, g * LANES:(g + 1) * LANES]
        o_ref[:, g * LANES:(g + 1) * LANES] = (xg * cos + pltpu.roll(xg, LANES // 2, 1) * sin).astype(o_ref.dtype)


def qk_proj_rope(cfg, h, w_qk, cos_t, sin_t):
    D = cfg.D
    N = w_qk.shape[1]
    tm = _tile(cfg.T, 512)
    tn = _tile(D, 512)
    tpb = cfg.T // tm
    qscale = cfg.da_dim ** -0.5 * math.log2(math.e)
    tbl = pl.BlockSpec((tm, LANES), lambda i, j: (i % tpb, 0))
    return pl.pallas_call(
        functools.partial(_qk_rope_kernel, q_tiles=D // tn, qscale=qscale),
        grid=(cfg.n_lat // tm, N // tn),
        in_specs=[pl.BlockSpec((tm, D), lambda i, j: (i, 0)), pl.BlockSpec((D, tn), lambda i, j: (0, j)), tbl, tbl],
        out_specs=pl.BlockSpec((tm, tn), lambda i, j: (i, j)),
        out_shape=jax.ShapeDtypeStruct((cfg.n_lat, N), BF16),
        compiler_params=_cparams(("parallel", "arbitrary")),
        name="qk_proj_rope",
    )(h, w_qk, cos_t, sin_t)


def rope_col_order(cfg, w):
    K, N = w.shape
    q = cfg.da_dim // 4
    return w.reshape(K, N // cfg.da_dim, 2, 2, q).transpose(0, 1, 3, 2, 4).reshape(K, N)


def rope_tables(cfg):
    half = cfg.da_dim // 2
    t = jnp.arange(cfg.T)
    row = (t // cfg.grid_w).astype(F32)
    col = (t % cfg.grid_w).astype(F32)
    inv = ROPE_BASE ** (-jnp.arange(0, half, 2, dtype=F32) / half)
    ar, ac = row[:, None] * inv, col[:, None] * inv
    cos_t = jnp.concatenate([jnp.cos(ar), jnp.cos(ac), jnp.cos(ar), jnp.cos(ac)], axis=1)
    sin_t = jnp.concatenate([-jnp.sin(ar), -jnp.sin(ac), jnp.sin(ar), jnp.sin(ac)], axis=1)
    return cos_t, sin_t


def _attn_kernel(*refs, tk, n_lat, lam_init, dd):
    if n_lat:
        (lam_ref, g_ref, q_ref, kc_ref, vc_ref, kl_ref, vl_ref, o_ref, acc_ref, m_ref, l_ref, sa_ref,
         sb_ref) = refs
    else:
        lam_ref, g_ref, q_ref, kc_ref, vc_ref, o_ref, acc_ref, m_ref, l_ref = refs
    d = dd // 2
    q = q_ref[...]

    m_ref[...] = jnp.full(m_ref.shape, -jnp.inf, F32)
    l_ref[...] = jnp.zeros(l_ref.shape, F32)
    acc_ref[...] = jnp.zeros(acc_ref.shape, F32)

    def score(k, p):
        return lax.dot_general(q[:, p * d:(p + 1) * d], k[:, p * d:(p + 1) * d], (((1,), (1,)), ((), ())),
                               preferred_element_type=F32)

    def softmax_pv(s_of, v):
        n = v.shape[0]
        nh = n // 2 if (n // 2) % LANES == 0 else n
        for p in range(2):
            m_prev = m_ref[p]
            m_next = jnp.maximum(m_prev, jnp.max(s_of(p, slice(0, n)), axis=1)[:, None])
            alpha = jnp.exp2(m_prev - m_next)
            m_rep = _lane_tile(m_next, nh // LANES)
            l_new = alpha * l_ref[p]
            acc = acc_ref[p] * _lane_tile(alpha, dd // LANES)
            for c0 in range(0, n, nh):
                pr = jnp.exp2(s_of(p, slice(c0, c0 + nh)) - m_rep)
                l_new = l_new + jnp.sum(pr, axis=1)[:, None]
                acc = acc + jnp.dot(pr.astype(BF16), v[c0:c0 + nh, :], preferred_element_type=F32)
            l_ref[p] = l_new
            m_ref[p] = m_next
            acc_ref[p] = acc

    kc = kc_ref[...]
    sc = [score(kc, p) for p in range(2)]
    if not n_lat:
        softmax_pv(lambda p, cols: sc[p][:, cols], vc_ref[...])
    else:
        def chunk(ref, j):
            return ref[pl.ds(pl.multiple_of(j * tk, tk), tk), :]

        def scores_to(s_ref, j):
            k = chunk(kl_ref, j)
            for p in range(2):
                s_ref[p] = score(k, p)

        scores_to(sa_ref, 0)
        softmax_pv(lambda p, cols: sc[p][:, cols], vc_ref[...])

        def body(jj, carry):
            j0 = 2 * jj
            scores_to(sb_ref, j0 + 1)
            softmax_pv(lambda p, cols: sa_ref[p, :, cols],chunk(vl_ref, j0))
            scores_to(sa_ref, j0 + 2)
            softmax_pv(lambda p, cols: sb_ref[p, :, cols],chunk(vl_ref, j0 + 1))
            return carry

        lax.fori_loop(0, n_lat // 2 - 1, body, 0)
        scores_to(sb_ref, n_lat - 1)
        softmax_pv(lambda p, cols: sa_ref[p, :, cols],chunk(vl_ref, n_lat - 2))
        softmax_pv(lambda p, cols: sb_ref[p, :, cols],chunk(vl_ref, n_lat - 1))

    lv = lam_ref[...]
    lam = (jnp.exp(jnp.sum(lv[0:1] * lv[1:2], axis=1, keepdims=True))
           - jnp.exp(jnp.sum(lv[2:3] * lv[3:4], axis=1, keepdims=True)) + lam_init)
    o1 = acc_ref[0] / _lane_tile(l_ref[0], dd // LANES)
    o2 = acc_ref[1] / _lane_tile(l_ref[1], dd // LANES)
    o = o1 - lam * o2
    o = _rms(o, g_ref[...]) * (1.0 - lam_init)
    o_ref[...] = o.astype(o_ref.dtype)


def diff_attention(cfg, qk_lat, qk_ctx, v, lam_vecs, subln_g, lam_init, *, latent):
    D, B, T, Tc, H = cfg.D, cfg.B, cfg.T, cfg.Tc, cfg.da_heads
    dd = 2 * cfg.da_dim
    assert T % Tc == 0 and dd % LANES == 0 and Tc % LANES == 0
    ctx_blk0 = cfg.n_lat // Tc
    small = lambda shape: pl.BlockSpec(shape, lambda b, h, i: (0, 0))
    kv_ctx = [pl.BlockSpec((Tc, dd), lambda b, h, i: (b, H + h)),
              pl.BlockSpec((Tc, dd), lambda b, h, i: (ctx_blk0 + b, h))]
    if latent:
        tq, tk = _tile(T, 512), _tile(T // 2, 1024)
        nq, n_lat = T // tq, T // tk
        assert n_lat % 2 == 0
        in_specs = [small((4, cfg.da_dim)), small((1, dd)),
                    pl.BlockSpec((tq, dd), lambda b, h, i: (b * nq + i, h))] + kv_ctx + [
            pl.BlockSpec((T, dd), lambda b, h, i: (b, H + h)),
            pl.BlockSpec((T, dd), lambda b, h, i: (b, h))]
        args = [lam_vecs, subln_g, qk_lat, qk_ctx, v, qk_lat, v]
        extra = [pltpu.VMEM((2, tq, tk), F32), pltpu.VMEM((2, tq, tk), F32)]
    else:
        tq, tk, nq, n_lat = Tc, 0, 1, 0
        in_specs = [small((4, cfg.da_dim)), small((1, dd)),
                    pl.BlockSpec((tq, dd), lambda b, h, i: (b, h))] + kv_ctx
        args = [lam_vecs, subln_g, qk_ctx, qk_ctx, v]
        extra = []
    scratch = [pltpu.VMEM((2, tq, dd), F32), pltpu.VMEM((2, tq, LANES), F32), pltpu.VMEM((2, tq, LANES), F32)]
    return pl.pallas_call(
        functools.partial(_attn_kernel, tk=tk, n_lat=n_lat, lam_init=lam_init, dd=dd),
        grid=(B, H, nq),
        in_specs=in_specs,
        out_specs=pl.BlockSpec((tq, dd), lambda b, h, i: (b * nq + i, h)),
        out_shape=jax.ShapeDtypeStruct((B * (T if latent else Tc), D), BF16),
        scratch_shapes=scratch + extra,
        compiler_params=_cparams(("parallel", "parallel", "arbitrary")),
        name="diff_attn_lat" if latent else "diff_attn_ctx",
    )(*args)


def _time_cumsum(x, reverse):
    L = x.shape[0]
    row = lax.broadcasted_iota(jnp.int32, x.shape, 0)
    sh = 1
    while sh < L:
        if reverse:
            x = x + jnp.where(row < L - sh, pltpu.roll(x, L - sh, 0), 0.0)
        else:
            x = x + jnp.where(row >= sh, pltpu.roll(x, sh, 0), 0.0)
        sh *= 2
    return x


def _mlstm_kernel(*refs, H, dqk, dv, L, reverse, finish, gate_off):
    if finish:
        (q_ref, k_ref, v_ref, gt_ref, bias_ref, hf_ref, og_ref, ng_ref, o_ref, c_ref, n_ref, m_ref) = refs
    else:
        (q_ref, k_ref, v_ref, gt_ref, bias_ref, o_ref, c_ref, n_ref, m_ref) = refs

    @pl.when(pl.program_id(1) == 0)
    def _():
        c_ref[...] = jnp.zeros(c_ref.shape, F32)
        n_ref[...] = jnp.zeros(n_ref.shape, F32)
        m_ref[...] = jnp.zeros(m_ref.shape, F32)

    gates = GATE_CAP * jnp.tanh((gt_ref[...] + bias_ref[...]) * (1.0 / GATE_CAP))
    ig = pltpu.roll(gates, LANES - gate_off, 1) if gate_off else gates
    fg = pltpu.roll(gates, LANES - gate_off - H, 1)
    lf = jnp.minimum(fg, 0.0) - jnp.log1p(jnp.exp(-jnp.abs(fg)))
    bc = _time_cumsum(lf, reverse)
    m_prev = m_ref[0:1, :]
    a_all = bc + m_prev
    b_last = bc[0:1, :] if reverse else bc[L - 1:L, :]
    g_all = b_last - bc + ig
    m_new = jnp.maximum(b_last + m_prev, jnp.max(g_all, axis=0, keepdims=True))
    decay = jnp.exp(b_last + m_prev - m_new)
    eg = jnp.exp(g_all - m_new)
    xt = (ig - bc).T
    m_ref[...] = jnp.broadcast_to(m_new, m_ref.shape)

    li = lax.broadcasted_iota(jnp.int32, (L, L), 0)
    si = lax.broadcasted_iota(jnp.int32, (L, L), 1)
    tri = (si >= li) if reverse else (si <= li)
    qscale = dqk ** -0.5

    for h in range(H):
        q = (q_ref[:, h * dqk:(h + 1) * dqk].astype(F32) * qscale).astype(BF16)
        k = k_ref[:, h * dqk:(h + 1) * dqk]
        v = v_ref[:, h * dv:(h + 1) * dv]
        ct = c_ref[h]
        nv = n_ref[h]
        a_h = a_all[:, h:h + 1]
        dmat = jnp.where(tri, bc[:, h:h + 1] + xt[h:h + 1, :], -jnp.inf)
        m_t = jnp.maximum(a_h, jnp.max(dmat, axis=1, keepdims=True))
        qk = lax.dot_general(q, k, (((1,), (1,)), ((), ())), preferred_element_type=F32)
        w = jnp.exp(dmat - m_t) * qk
        inter = jnp.exp(a_h - m_t)
        num = inter * jnp.dot(q, ct.astype(BF16), preferred_element_type=F32) + jnp.dot(
            w.astype(BF16), v, preferred_element_type=F32)
        qn = jnp.sum(q.astype(F32) * nv, axis=1, keepdims=True)
        den = inter * qn + jnp.sum(w, axis=1, keepdims=True)
        hout = num / jnp.maximum(jnp.abs(den), jnp.exp(-m_t))

        dec_h = decay[:, h:h + 1]
        wk = eg[:, h:h + 1] * k.astype(F32)
        c_ref[h] = dec_h * ct + lax.dot_general(wk.astype(BF16), v, (((0,), (0,)), ((), ())),
                                                preferred_element_type=F32)
        n_ref[h] = dec_h * nv + jnp.sum(wk, axis=0, keepdims=True)

        if finish:
            hs = hout + hf_ref[:, h * dv:(h + 1) * dv]
            hn = _rms(hs, ng_ref[:, h * dv:(h + 1) * dv]).astype(BF16)
            og = og_ref[:, h * dv:(h + 1) * dv]
            o_ref[:, h * dv:(h + 1) * dv] = (hn * jax.nn.sigmoid(og)).astype(o_ref.dtype)
        else:
            o_ref[:, h * dv:(h + 1) * dv] = hout.astype(o_ref.dtype)


def mlstm_scan(cfg, proj, gates, bias, *, reverse, hf=None, norm_g=None):
    D, B, T, Tc, H, L = cfg.D, cfg.B, cfg.T, cfg.Tc, cfg.ml_heads, cfg.chunk
    dqk, dv = cfg.ml_qk, cfg.ml_v
    ncc, ncl = Tc // L, T // L
    nc = ncc + ncl
    finish = hf is not None
    qw = H * dqk
    assert D % qw == 0 and (H * dv) == D

    def rb(b, c):
        if reverse:
            return jnp.where(c < ncc, cfg.n_lat // L + b * ncc + (ncc - 1 - c), b * ncl + (nc - 1 - c))
        return jnp.where(c < ncc, cfg.n_lat // L + b * ncc + c, b * ncl + (c - ncc))

    in_specs = [
        pl.BlockSpec((L, qw), lambda b, c: (rb(b, c), 0)),
        pl.BlockSpec((L, qw), lambda b, c: (rb(b, c), 1)),
        pl.BlockSpec((L, D), lambda b, c: (rb(b, c), 2 * qw // D)),
        pl.BlockSpec((L, LANES), lambda b, c: (rb(b, c), 0)),
        pl.BlockSpec((1, LANES), lambda b, c: (0, 0)),
    ]
    args = [proj, proj, proj, gates, bias]
    if finish:
        in_specs += [
            pl.BlockSpec((L, D), lambda b, c: (rb(b, c), 0)),
            pl.BlockSpec((L, D), lambda b, c: (rb(b, c), 2 * qw // D + 1)),
            pl.BlockSpec((1, D), lambda b, c: (0, 0)),
        ]
        args += [hf, proj, norm_g]
    return pl.pallas_call(
        functools.partial(_mlstm_kernel, H=H, dqk=dqk, dv=dv, L=L, reverse=reverse, finish=finish,
                          gate_off=2 * H if reverse else 0),
        grid=(B, nc),
        in_specs=in_specs,
        out_specs=pl.BlockSpec((L, D), lambda b, c: (rb(b, c), 0)),
        out_shape=jax.ShapeDtypeStruct((cfg.R, D), BF16 if finish else F32),
        scratch_shapes=[pltpu.VMEM((H, dqk, dv), F32), pltpu.VMEM((H, 1, dqk), F32), pltpu.VMEM((8, LANES), F32)],
        compiler_params=_cparams(("parallel", "arbitrary")),
        name="mlstm_bwd" if reverse else "mlstm_fwd",
    )(*args)


def _moe_up_kernel(x_ref, wg_ref, wu_ref, o_ref, wgb_ref, wub_ref):
    @pl.when(pl.program_id(2) == 0)
    def _():
        wgb_ref[...] = wg_ref[...].astype(BF16)
        wub_ref[...] = wu_ref[...].astype(BF16)

    x = x_ref[...]
    a = jnp.dot(x, wgb_ref[...], preferred_element_type=F32)
    u = jnp.dot(x, wub_ref[...], preferred_element_type=F32)
    o_ref[...] = (a * jax.nn.sigmoid(a) * u).astype(o_ref.dtype)


def _row_tile(n, cap):
    best = n
    for t in range(16, min(n, cap) + 1, 16):
        if n % t == 0:
            best = t
    return best


def moe_up(cfg, xe, w_gate, w_up, layer):
    E, P, D = xe.shape
    F = cfg.F
    tp = _row_tile(P, 1056)
    fk = _tile(F, 256)
    w_spec = pl.BlockSpec((None, None, D, fk), lambda e, f, p: (layer, e, 0, f))
    return pl.pallas_call(
        _moe_up_kernel,
        grid=(E, F // fk, P // tp),
        in_specs=[pl.BlockSpec((None, tp, D), lambda e, f, p: (e, p, 0)), w_spec, w_spec],
        out_specs=pl.BlockSpec((None, tp, fk), lambda e, f, p: (e, p, f)),
        out_shape=jax.ShapeDtypeStruct((E, P, F), BF16),
        scratch_shapes=[pltpu.VMEM((D, fk), BF16), pltpu.VMEM((D, fk), BF16)],
        compiler_params=_cparams(("parallel", "parallel", "arbitrary")),
        name="moe_up",
    )(xe, w_gate, w_up)


def _moe_down_kernel(rows_ref, hid_ref, g_ref, wd_ref, acc_in_ref, acc_ref, wdb_ref, buf_ref, sem_in, sem_out, *,
                     tp, P, nblk):
    del acc_in_ref
    e, p = pl.program_id(0), pl.program_id(1)
    step = e * pl.num_programs(1) + p
    last = pl.num_programs(0) * pl.num_programs(1) - 1
    slot = lax.rem(step, 2)
    base = e * P + p * tp
    blk = tp // nblk

    def tile_copy(s, sem):
        return pltpu.make_async_copy(acc_ref.at[pl.ds(0, tp)], buf_ref.at[s], sem)

    @pl.when(p == 0)
    def _():
        wdb_ref[...] = wd_ref[...].astype(BF16)

        @pl.when(step > 0)
        def _():
            tile_copy(1 - slot, sem_out.at[1 - slot]).wait()

    ys = []
    for i in range(nblk):
        r0 = i * blk
        ys.append(jnp.dot(hid_ref[r0:r0 + blk, :], wdb_ref[...], preferred_element_type=F32)
                  * g_ref[r0:r0 + blk, :])
        for r in range(r0, r0 + blk):
            pltpu.make_async_copy(acc_ref.at[pl.ds(rows_ref[base + r], 1)], buf_ref.at[slot, pl.ds(r, 1)],
                                  sem_in).start()
    tile_copy(slot, sem_in).wait()
    for i in range(nblk):
        r0 = i * blk
        buf_ref[slot, r0:r0 + blk, :] += ys[i]
        for r in range(r0, r0 + blk):
            pltpu.make_async_copy(buf_ref.at[slot, pl.ds(r, 1)], acc_ref.at[pl.ds(rows_ref[base + r], 1)],
                                  sem_out.at[slot]).start()

    @pl.when(jnp.logical_and(p != 0, step > 0))
    def _():
        tile_copy(1 - slot, sem_out.at[1 - slot]).wait()

    @pl.when(step == last)
    def _():
        tile_copy(slot, sem_out.at[slot]).wait()


def moe_down_scatter(cfg, rows, hid, ge, w_down, layer, n_rows):
    E, P, F = hid.shape
    D = cfg.D
    tp = _row_tile(P, 352)
    grid_spec = pltpu.PrefetchScalarGridSpec(
        num_scalar_prefetch=1,
        grid=(E, P // tp),
        in_specs=[
            pl.BlockSpec((None, tp, F), lambda e, p, rows: (e, p, 0)),
            pl.BlockSpec((None, tp, 1), lambda e, p, rows: (e, p, 0)),
            pl.BlockSpec((None, None, F, D), lambda e, p, rows: (layer, e, 0, 0), pipeline_mode=pl.Buffered(1)),
            pl.BlockSpec(memory_space=pl.ANY),
        ],
        out_specs=pl.BlockSpec(memory_space=pl.ANY),
        scratch_shapes=[pltpu.VMEM((F, D), BF16), pltpu.VMEM((2, tp, D), F32), pltpu.SemaphoreType.DMA(()),
                        pltpu.SemaphoreType.DMA((2,))],
    )
    nblk = 2 if tp % 32 == 0 else 1
    return pl.pallas_call(
        functools.partial(_moe_down_kernel, tp=tp, P=P, nblk=nblk),
        grid_spec=grid_spec,
        out_shape=jax.ShapeDtypeStruct((n_rows, D), F32),
        input_output_aliases={4: 0},
        compiler_params=_cparams(("arbitrary", "arbitrary")),
        name="moe_down_scatter",
    )(rows.reshape(-1), hid, ge, w_down, jnp.zeros((n_rows, D), F32))


def expert_choice_moe(cfg, h, logits, w_gate, w_up, w_down, layer, with_ctx):
    B, T, Tc, E = cfg.B, cfg.T, cfg.Tc, cfg.E
    n_rows = cfg.R if with_ctx else cfg.n_lat
    aff = jax.nn.softmax(logits[:n_rows, :E], axis=-1)
    sets = [(b * T, T) for b in range(B)]
    if with_ctx:
        sets += [(cfg.n_lat + b * Tc, Tc) for b in range(B)]
    rows, gates = [], []
    for start, n in sets:
        cap = max(1, (cfg.cap_factor * n) // E)
        g, idx = lax.top_k(aff[start:start + n].T, cap)
        rows.append(idx + start)
        gates.append(g)
    rows = jnp.concatenate(rows, axis=1)
    gates = jnp.concatenate(gates, axis=1)
    xe = h[rows]
    hid = moe_up(cfg, xe, w_gate, w_up, layer)
    return moe_down_scatter(cfg, rows, hid, gates[..., None], w_down, layer, n_rows)


def _forward(cfg, x, c, ctx, c_ctx, ada_w, ada_b, norm_g, attn_w_in, attn_w_out, attn_lambda, attn_subln_g,
             mlstm_w_in, mlstm_b_gates, mlstm_norm_g, mlstm_w_out, router_w, expert_w_gate, expert_w_up,
             expert_w_down):
    D, B, T, Tc, E = cfg.D, cfg.B, cfg.T, cfg.Tc, cfg.E
    depth = ada_w.shape[0]
    cos_t, sin_t = rope_tables(cfg)
    cvec = jnp.concatenate([c, c_ctx[None, :], jnp.zeros((8 - B - 1, D), F32)], axis=0)
    ada_b3 = ada_b.reshape(depth, 1, 6 * D)
    norm_g3 = norm_g.reshape(depth * 4, 1, D)
    X = (x.reshape(B * T, D), ctx.reshape(B * Tc, D))

    mods = [adaln(cvec, ada_w, ada_b3, i)[:B + 1].reshape((B + 1) * 6, 1, D) for i in range(depth)]
    rw = jnp.pad(router_w, ((0, 0), (0, 0), (0, LANES - E)))
    rw_hi = rw.astype(BF16)
    rw_lo = (rw - rw_hi.astype(F32)).astype(BF16)

    _, h, _ = resid_norm(cfg, X, norm_g3, h_part=(mods[0], 0, 0, 1, 0))
    for i in range(depth):
        last = i == depth - 1
        mod = mods[i]
        j = i // 2
        n_rows = cfg.n_lat if last else cfg.R
        if i % 2 == 0:
            lam_init = 0.8 - 0.6 * math.exp(-0.3 * i)
            w_in = attn_w_in[j]
            w_qk = rope_col_order(cfg, w_in[:, :2 * D]).astype(BF16)
            qk_lat = qk_proj_rope(cfg, h, w_qk, cos_t, sin_t)
            qk_ctx = matmul(h, w_qk, BF16, row0=cfg.n_lat, scale_cols=D,
                            scale=cfg.da_dim ** -0.5 * math.log2(math.e))
            v = matmul(h, w_in[:, 2 * D:].astype(BF16), BF16)
            g2 = attn_subln_g[j].reshape(1, -1)
            o = diff_attention(cfg, qk_lat, qk_ctx, v, attn_lambda[j], g2, lam_init, latent=True)
            w_out = attn_w_out[j].astype(BF16)
            if last:
                y = matmul(o, w_out, BF16)
            else:
                o_ctx = diff_attention(cfg, qk_lat, qk_ctx, v, attn_lambda[j], g2, lam_init, latent=False)
                y = matmul_stacked(o, o_ctx, w_out, BF16)
        else:
            H = cfg.ml_heads
            w_in = mlstm_w_in[j]
            nmain = w_in.shape[1] - 4 * H
            proj = matmul(h, w_in[:, :nmain].astype(BF16), BF16)
            wg = jnp.pad(w_in[:, nmain:], ((0, 0), (0, LANES - 4 * H))).astype(BF16)
            gates = matmul(h, wg, F32)
            bias = jnp.pad(mlstm_b_gates[j].reshape(1, 4 * H), ((0, 0), (0, LANES - 4 * H)))
            hf = mlstm_scan(cfg, proj, gates, bias, reverse=False)
            hn = mlstm_scan(cfg, proj, gates, bias, reverse=True, hf=hf, norm_g=mlstm_norm_g[j].reshape(1, D))
            y = matmul(hn, mlstm_w_out[j].astype(BF16), BF16, n_rows=n_rows)
        X, h, lg = resid_norm(cfg, X, norm_g3, y_part=(y, mod, i, 2, 1), h_part=(mod, i, 2, 4, 3),
                              router=(rw_hi[i], rw_lo[i]), n_rows=n_rows)
        mo = expert_choice_moe(cfg, h, lg, expert_w_gate, expert_w_up, expert_w_down, i, with_ctx=not last)
        if last:
            X, _, _ = resid_norm(cfg, X, norm_g3, y_part=(mo, mod, i, 5, 3), n_rows=n_rows)
        else:
            X, h, _ = resid_norm(cfg, X, norm_g3, y_part=(mo, mod, i, 5, 3), h_part=(mods[i + 1], i + 1, 0, 1, 0))
    return X.reshape(B, T, D)


def kernel(x, c, ctx, c_ctx, ada_w, ada_b, norm_g, attn_w_in, attn_w_out, attn_lambda, attn_subln_g, mlstm_w_in,
           mlstm_b_gates, mlstm_norm_g, mlstm_w_out, router_w, expert_w_gate, expert_w_up, expert_w_down):
    return _forward(FULL_CFG, x, c, ctx, c_ctx, ada_w, ada_b, norm_g, attn_w_in, attn_w_out, attn_lambda,
                    attn_subln_g, mlstm_w_in, mlstm_b_gates, mlstm_norm_g, mlstm_w_out, router_w, expert_w_gate,
                    expert_w_up, expert_w_down)
```

```python
import functools
import math
from typing import NamedTuple

import jax
import jax.numpy as jnp
from jax import lax
from jax.experimental import pallas as pl
from jax.experimental.pallas import tpu as pltpu

BF16 = jnp.bfloat16
F32 = jnp.float32

EPS = 1e-6
GATE_CAP = 15.0
ROPE_BASE = 10000.0
LANES = 128
MIB = 1024 * 1024
VMEM_LIMIT = 56 * MIB

class Cfg(NamedTuple):
    D: int
    B: int
    T: int
    Tc: int
    grid_w: int
    da_heads: int
    da_dim: int
    ml_heads: int
    E: int
    F: int
    chunk: int
    cap_factor: int

    @property
    def n_lat(self):
        return self.B * self.T

    @property
    def n_ctx(self):
        return self.B * self.Tc

    @property
    def R(self):
        return self.n_lat + self.n_ctx

    @property
    def ml_qk(self):
        return self.D // 2 // self.ml_heads

    @property
    def ml_v(self):
        return self.D // self.ml_heads


FULL_CFG = Cfg(D=4096, B=2, T=8192, Tc=256, grid_w=64, da_heads=16, da_dim=128, ml_heads=8, E=16, F=1024,
               chunk=256, cap_factor=2)


def _cparams(sem, vmem=VMEM_LIMIT):
    return pltpu.CompilerParams(dimension_semantics=sem, vmem_limit_bytes=vmem)


def _tile(n, pref):
    t = min(n, pref)
    while n % t:
        t //= 2
    return t


def _lane_tile(x, reps):
    return x if reps == 1 else jnp.concatenate([x] * reps, axis=1)


def _adaln_kernel(c_ref, w_ref, b_ref, o_ref):
    c = c_ref[...]
    s = (c * jax.nn.sigmoid(c)).astype(BF16)
    o_ref[...] = jnp.dot(s, w_ref[...].astype(BF16), preferred_element_type=F32) + b_ref[...]


def adaln(cvec, ada_w, ada_b3, layer):
    D = cvec.shape[1]
    N = ada_w.shape[2]
    tn = _tile(N, 512)
    return pl.pallas_call(
        _adaln_kernel,
        grid=(N // tn,),
        in_specs=[
            pl.BlockSpec((8, D), lambda j: (0, 0)),
            pl.BlockSpec((None, D, tn), lambda j: (layer, 0, j)),
            pl.BlockSpec((None, 1, tn), lambda j: (layer, 0, j)),
        ],
        out_specs=pl.BlockSpec((8, tn), lambda j: (0, j)),
        out_shape=jax.ShapeDtypeStruct((8, N), F32),
        compiler_params=_cparams(("parallel",)),
        name="adaln",
    )(cvec, ada_w, ada_b3)


def _rms(x, g):
    return x * lax.rsqrt(jnp.mean(x * x, axis=-1, keepdims=True) + EPS) * g


def _resid_norm_kernel(*refs, split_at, has_y, want_h, want_logits):
    it = iter(refs)
    x_ref = next(it)
    xc_ref = next(it) if split_at else None
    if has_y:
        y_ref, gate_ref, gpost_ref = next(it), next(it), next(it)
    if want_h:
        gpre_ref, sc_ref, sh_ref = next(it), next(it), next(it)
    if want_logits:
        rwh_ref, rwl_ref = next(it), next(it)
    if has_y:
        xo_ref = next(it)
    if want_h:
        h_ref = next(it)
    if want_logits:
        lg_ref = next(it)

    x = x_ref[...]
    if split_at:
        x = jnp.where(pl.program_id(0) < split_at, x, xc_ref[...])
    if has_y:
        y = y_ref[...].astype(F32)
        x = x + gate_ref[0] * _rms(y, gpost_ref[0])
        xo_ref[...] = x
    if want_h:
        h = _rms(x, gpre_ref[0]) * (1.0 + sc_ref[0]) + sh_ref[0]
        hb = h.astype(BF16)
        h_ref[...] = hb
        if want_logits:
            hl = (h - hb.astype(F32)).astype(BF16)
            lg_ref[...] = (jnp.dot(hb, rwh_ref[...], preferred_element_type=F32)
                           + jnp.dot(hl, rwh_ref[...], preferred_element_type=F32)
                           + jnp.dot(hb, rwl_ref[...], preferred_element_type=F32))


def resid_norm(cfg, x, norm_g, *, y_part=None, h_part=None, router=None, n_rows=None):
    D = cfg.D
    tm = _tile(cfg.Tc, 128)
    count = n_rows if n_rows is not None else cfg.R
    assert count % tm == 0
    nlat_t = cfg.n_lat // tm
    tpb = cfg.T // tm
    has_y, want_h, want_logits = y_part is not None, h_part is not None, router is not None

    def set_of(i):
        return jnp.where(i < nlat_t, i // tpb, cfg.B)

    row_spec = lambda w: pl.BlockSpec((tm, w), lambda i: (i, 0))
    mod_spec = lambda k: pl.BlockSpec((1, 1, D), lambda i: (set_of(i) * 6 + k, 0, 0))
    g_spec = lambda layer, k: pl.BlockSpec((1, 1, D), lambda i: (layer * 4 + k, 0, 0))

    split = isinstance(x, tuple)
    if split:
        args = list(x)
        in_specs = [pl.BlockSpec((tm, D), lambda i: (jnp.minimum(i, nlat_t - 1), 0)),
                    pl.BlockSpec((tm, D), lambda i: (jnp.maximum(i - nlat_t, 0), 0))]
    else:
        args, in_specs = [x], [row_spec(D)]
    out_shape, out_specs = [], []
    if has_y:
        y, mod, layer, gate_k, gpost_k = y_part
        args += [y, mod, norm_g]
        in_specs += [row_spec(D), mod_spec(gate_k), g_spec(layer, gpost_k)]
    if want_h:
        mod, layer, gpre_k, sc_k, sh_k = h_part
        args += [norm_g, mod, mod]
        in_specs += [g_spec(layer, gpre_k), mod_spec(sc_k), mod_spec(sh_k)]
    if want_logits:
        args += [router[0], router[1]]
        in_specs += [pl.BlockSpec((D, LANES), lambda i: (0, 0))] * 2
    if has_y:
        out_shape.append(jax.ShapeDtypeStruct((count, D), F32))
        out_specs.append(row_spec(D))
    if want_h:
        out_shape.append(jax.ShapeDtypeStruct((count, D), BF16))
        out_specs.append(row_spec(D))
    if want_logits:
        out_shape.append(jax.ShapeDtypeStruct((count, LANES), F32))
        out_specs.append(row_spec(LANES))

    outs = pl.pallas_call(
        functools.partial(_resid_norm_kernel, split_at=nlat_t if split else 0, has_y=has_y, want_h=want_h,
                          want_logits=want_logits),
        grid=(count // tm,),
        in_specs=in_specs,
        out_specs=out_specs,
        out_shape=out_shape,
        compiler_params=_cparams(("parallel",)),
        name="resid_norm",
    )(*args)
    outs = list(outs)
    x_new = outs.pop(0) if has_y else None
    h = outs.pop(0) if want_h else None
    lg = outs.pop(0) if want_logits else None
    return x_new, h, lg


def _mm_kernel(x_ref, w_ref, o_ref, *, scale_tiles, scale):
    acc = jnp.dot(x_ref[...], w_ref[...], preferred_element_type=F32)
    if scale_tiles:
        acc = acc * jnp.where(pl.program_id(1) < scale_tiles, scale, 1.0).astype(F32)
    o_ref[...] = acc.astype(o_ref.dtype)


def matmul(x, w, out_dtype, n_rows=None, row0=0, scale_cols=0, scale=1.0, tm_pref=1024, tn_pref=1024):
    K = x.shape[1]
    M = n_rows if n_rows is not None else x.shape[0] - row0
    N = w.shape[1]
    tm, tn = _tile(math.gcd(M, row0), tm_pref), _tile(math.gcd(N, scale_cols), tn_pref)
    assert row0 % tm == 0 and scale_cols % tn == 0
    r0 = row0 // tm
    return pl.pallas_call(
        functools.partial(_mm_kernel, scale_tiles=scale_cols // tn, scale=scale),
        grid=(M // tm, N // tn),
        in_specs=[pl.BlockSpec((tm, K), lambda i, j: (i + r0, 0)), pl.BlockSpec((K, tn), lambda i, j: (0, j))],
        out_specs=pl.BlockSpec((tm, tn), lambda i, j: (i, j)),
        out_shape=jax.ShapeDtypeStruct((M, N), out_dtype),
        compiler_params=_cparams(("parallel", "arbitrary")),
        name="matmul",
    )(x, w)


def _mm_wcast_kernel(*refs, na_tiles):
    if na_tiles:
        xa_ref, xb_ref, w_ref, o_ref, wb_ref = refs
    else:
        xa_ref, w_ref, o_ref, wb_ref = refs

    @pl.when(pl.program_id(1) == 0)
    def _():
        wb_ref[...] = w_ref[...].astype(BF16)

    x = xa_ref[...]
    if na_tiles:
        x = jnp.where(pl.program_id(1) < na_tiles, x, xb_ref[...])
    o_ref[...] = jnp.dot(x, wb_ref[...], preferred_element_type=F32).astype(o_ref.dtype)


def matmul_wcast(x, w, lead, out_dtype, n_rows=None, col0=0, n_cols=None, tm_pref=512, tn_pref=1024):
    xs = x if isinstance(x, tuple) else (x,)
    K = xs[0].shape[1]
    N = n_cols if n_cols is not None else w.shape[2]
    if len(xs) == 2:
        Ma, Mb = xs[0].shape[0], xs[1].shape[0]
        M = Ma + Mb
        tm = _tile(math.gcd(Ma, Mb), tm_pref)
        na = Ma // tm
        x_specs = [pl.BlockSpec((tm, K), lambda j, i: (jnp.minimum(i, na - 1), 0)),
                   pl.BlockSpec((tm, K), lambda j, i: (jnp.maximum(i - na, 0), 0))]
    else:
        M = n_rows if n_rows is not None else xs[0].shape[0]
        tm, na = _tile(M, tm_pref), 0
        x_specs = [pl.BlockSpec((tm, K), lambda j, i: (i, 0))]
    tn = _tile(math.gcd(N, col0), tn_pref)
    c0 = col0 // tn
    return pl.pallas_call(
        functools.partial(_mm_wcast_kernel, na_tiles=na),
        grid=(N // tn, M // tm),
        in_specs=x_specs + [pl.BlockSpec((None, K, tn), lambda j, i: (lead, 0, j + c0),
                                         pipeline_mode=pl.Buffered(1))],
        out_specs=pl.BlockSpec((tm, tn), lambda j, i: (i, j)),
        out_shape=jax.ShapeDtypeStruct((M, N), out_dtype),
        scratch_shapes=[pltpu.VMEM((K, tn), BF16)],
        compiler_params=_cparams(("parallel", "arbitrary")),
        name="matmul_wcast",
    )(*xs, w)


def _qk_rope_kernel(x_ref, w_ref, cos_ref, sin_ref, o_ref, *, q_tiles, qscale):
    scale = jnp.where(pl.program_id(1) < q_tiles, qscale, 1.0).astype(F32)
    cos, sin = cos_ref[...] * scale, sin_ref[...] * scale
    x = x_ref[...]
    tn = o_ref.shape[1]
    half = tn // 2 if (tn // 2) % LANES == 0 else tn
    for c0 in range(0, tn, half):
        acc = jnp.dot(x, w_ref[:, c0:c0 + half], preferred_element_type=F32)
        for g in range(half // LANES):
            xg = acc[:, g * LANES:(g + 1) * LANES]
            o_ref[:, c0 + g * LANES:c0 + (g + 1) * LANES] = (
                xg * cos + pltpu.roll(xg, LANES // 2, 1) * sin).astype(o_ref.dtype)


def qk_proj_rope(cfg, h, w_qk, cos_t, sin_t):
    D = cfg.D
    N = w_qk.shape[1]
    tm = _tile(cfg.T, 512)
    tn = _tile(D, 512)
    tpb = cfg.T // tm
    qscale = cfg.da_dim ** -0.5 * math.log2(math.e)
    tbl = pl.BlockSpec((tm, LANES), lambda i, j: (i % tpb, 0))
    return pl.pallas_call(
        functools.partial(_qk_rope_kernel, q_tiles=D // tn, qscale=qscale),
        grid=(cfg.n_lat // tm, N // tn),
        in_specs=[pl.BlockSpec((tm, D), lambda i, j: (i, 0)), pl.BlockSpec((D, tn), lambda i, j: (0, j)), tbl, tbl],
        out_specs=pl.BlockSpec((tm, tn), lambda i, j: (i, j)),
        out_shape=jax.ShapeDtypeStruct((cfg.n_lat, N), BF16),
        compiler_params=_cparams(("parallel", "arbitrary")),
        name="qk_proj_rope",
    )(h, w_qk, cos_t, sin_t)


def rope_col_order(cfg, w):
    K, N = w.shape
    q = cfg.da_dim // 4
    return w.reshape(K, N // cfg.da_dim, 2, 2, q).transpose(0, 1, 3, 2, 4).reshape(K, N)


def rope_tables(cfg):
    half = cfg.da_dim // 2
    t = jnp.arange(cfg.T)
    row = (t // cfg.grid_w).astype(F32)
    col = (t % cfg.grid_w).astype(F32)
    inv = ROPE_BASE ** (-jnp.arange(0, half, 2, dtype=F32) / half)
    ar, ac = row[:, None] * inv, col[:, None] * inv
    cos_t = jnp.concatenate([jnp.cos(ar), jnp.cos(ac), jnp.cos(ar), jnp.cos(ac)], axis=1)
    sin_t = jnp.concatenate([-jnp.sin(ar), -jnp.sin(ac), jnp.sin(ar), jnp.sin(ac)], axis=1)
    return cos_t, sin_t


def _attn_kernel(*refs, tk, n_lat, lam_init, dd):
    if n_lat:
        (lam_ref, g_ref, q_ref, kc_ref, vc_ref, kl_ref, vl_ref, o_ref, acc_ref, m_ref, l_ref, sa_ref,
         sb_ref) = refs
    else:
        lam_ref, g_ref, q_ref, kc_ref, vc_ref, o_ref, acc_ref, m_ref, l_ref = refs
    d = dd // 2
    q = q_ref[...]

    m_ref[...] = jnp.full(m_ref.shape, -jnp.inf, F32)
    l_ref[...] = jnp.zeros(l_ref.shape, F32)
    acc_ref[...] = jnp.zeros(acc_ref.shape, F32)

    tq = q.shape[0]
    every = slice(0, tq)

    def score(k, p, rows):
        return lax.dot_general(q[rows, p * d:(p + 1) * d], k[:, p * d:(p + 1) * d], (((1,), (1,)), ((), ())),
                               preferred_element_type=F32)

    def softmax_pv(s_of, v, rows):
        n = v.shape[0]
        nh = n // 2 if (n // 2) % LANES == 0 else n
        for p in range(2):
            m_prev = m_ref[p, rows]
            m_next = jnp.maximum(m_prev, jnp.max(s_of(p, slice(0, n)), axis=1)[:, None])
            alpha = jnp.exp2(m_prev - m_next)
            m_rep = _lane_tile(m_next, nh // LANES)
            l_new = alpha * l_ref[p, rows]
            acc = acc_ref[p, rows] * _lane_tile(alpha, dd // LANES)
            for c0 in range(0, n, nh):
                pr = jnp.exp2(s_of(p, slice(c0, c0 + nh)) - m_rep)
                l_new = l_new + jnp.sum(pr, axis=1)[:, None]
                acc = acc + jnp.dot(pr.astype(BF16), v[c0:c0 + nh, :], preferred_element_type=F32)
            l_ref[p, rows] = l_new
            m_ref[p, rows] = m_next
            acc_ref[p, rows] = acc

    kc = kc_ref[...]
    sc = [score(kc, p, every) for p in range(2)]
    if not n_lat:
        softmax_pv(lambda p, cols: sc[p][:, cols], vc_ref[...], every)
    else:
        half_a, half_b = slice(0, tq // 2), slice(tq // 2, tq)

        def chunk(ref, j):
            return ref[pl.ds(pl.multiple_of(j * tk, tk), tk), :]

        def scores_to(s_ref, j, rows):
            k = chunk(kl_ref, j)
            for p in range(2):
                s_ref[p] = score(k, p, rows)

        from_a = lambda p, cols: sa_ref[p, :, cols]
        from_b = lambda p, cols: sb_ref[p, :, cols]
        scores_to(sa_ref, 0, half_a)
        softmax_pv(lambda p, cols: sc[p][:, cols], vc_ref[...], every)

        def body(j, carry):
            scores_to(sb_ref, j, half_b)
            softmax_pv(from_a, chunk(vl_ref, j), half_a)
            scores_to(sa_ref, j + 1, half_a)
            softmax_pv(from_b, chunk(vl_ref, j), half_b)
            return carry

        lax.fori_loop(0, n_lat - 1, body, 0)
        scores_to(sb_ref, n_lat - 1, half_b)
        softmax_pv(from_a, chunk(vl_ref, n_lat - 1), half_a)
        softmax_pv(from_b, chunk(vl_ref, n_lat - 1), half_b)

    lv = lam_ref[...]
    lam = (jnp.exp(jnp.sum(lv[0:1] * lv[1:2], axis=1, keepdims=True))
           - jnp.exp(jnp.sum(lv[2:3] * lv[3:4], axis=1, keepdims=True)) + lam_init)
    o1 = acc_ref[0] / _lane_tile(l_ref[0], dd // LANES)
    o2 = acc_ref[1] / _lane_tile(l_ref[1], dd // LANES)
    o = o1 - lam * o2
    o = _rms(o, g_ref[...]) * (1.0 - lam_init)
    o_ref[...] = o.astype(o_ref.dtype)


def diff_attention(cfg, qk_lat, qk_ctx, v, lam_vecs, subln_g, lam_init, *, latent):
    D, B, T, Tc, H = cfg.D, cfg.B, cfg.T, cfg.Tc, cfg.da_heads
    dd = 2 * cfg.da_dim
    assert T % Tc == 0 and dd % LANES == 0 and Tc % LANES == 0
    ctx_blk0 = cfg.n_lat // Tc
    small = lambda shape: pl.BlockSpec(shape, lambda b, h, i: (0, 0))
    kv_ctx = [pl.BlockSpec((Tc, dd), lambda b, h, i: (b, H + h)),
              pl.BlockSpec((Tc, dd), lambda b, h, i: (ctx_blk0 + b, h))]
    if latent:
        tq, tk = _tile(T, 1024), _tile(T // 2, 1024)
        nq, n_lat = T // tq, T // tk
        assert (tq // 2) % 16 == 0
        in_specs = [small((4, cfg.da_dim)), small((1, dd)),
                    pl.BlockSpec((tq, dd), lambda b, h, i: (b * nq + i, h))] + kv_ctx + [
            pl.BlockSpec((T, dd), lambda b, h, i: (b, H + h)),
            pl.BlockSpec((T, dd), lambda b, h, i: (b, h))]
        args = [lam_vecs, subln_g, qk_lat, qk_ctx, v, qk_lat, v]
        extra = [pltpu.VMEM((2, tq // 2, tk), F32), pltpu.VMEM((2, tq // 2, tk), F32)]
    else:
        tq, tk, nq, n_lat = Tc, 0, 1, 0
        in_specs = [small((4, cfg.da_dim)), small((1, dd)),
                    pl.BlockSpec((tq, dd), lambda b, h, i: (b, h))] + kv_ctx
        args = [lam_vecs, subln_g, qk_ctx, qk_ctx, v]
        extra = []
    scratch = [pltpu.VMEM((2, tq, dd), F32), pltpu.VMEM((2, tq, LANES), F32), pltpu.VMEM((2, tq, LANES), F32)]
    return pl.pallas_call(
        functools.partial(_attn_kernel, tk=tk, n_lat=n_lat, lam_init=lam_init, dd=dd),
        grid=(B, H, nq),
        in_specs=in_specs,
        out_specs=pl.BlockSpec((tq, dd), lambda b, h, i: (b * nq + i, h)),
        out_shape=jax.ShapeDtypeStruct((B * (T if latent else Tc), D), BF16),
        scratch_shapes=scratch + extra,
        compiler_params=_cparams(("parallel", "parallel", "arbitrary")),
        name="diff_attn_lat" if latent else "diff_attn_ctx",
    )(*args)


def _time_cumsum(x, reverse):
    L = x.shape[0]
    row = lax.broadcasted_iota(jnp.int32, x.shape, 0)
    sh = 1
    while sh < L:
        if reverse:
            x = x + jnp.where(row < L - sh, pltpu.roll(x, L - sh, 0), 0.0)
        else:
            x = x + jnp.where(row >= sh, pltpu.roll(x, sh, 0), 0.0)
        sh *= 2
    return x


def _mlstm_kernel(*refs, H, dqk, dv, L, reverse, finish, gate_off):
    if finish:
        (q_ref, k_ref, v_ref, gt_ref, bias_ref, hf_ref, og_ref, ng_ref, o_ref, c_ref, n_ref, m_ref) = refs
    else:
        (q_ref, k_ref, v_ref, gt_ref, bias_ref, o_ref, c_ref, n_ref, m_ref) = refs

    @pl.when(pl.program_id(1) == 0)
    def _():
        c_ref[...] = jnp.zeros(c_ref.shape, F32)
        n_ref[...] = jnp.zeros(n_ref.shape, F32)
        m_ref[...] = jnp.zeros(m_ref.shape, F32)

    gates = GATE_CAP * jnp.tanh((gt_ref[...] + bias_ref[...]) * (1.0 / GATE_CAP))
    ig = pltpu.roll(gates, LANES - gate_off, 1) if gate_off else gates
    fg = pltpu.roll(gates, LANES - gate_off - H, 1)
    lf = jnp.minimum(fg, 0.0) - jnp.log1p(jnp.exp(-jnp.abs(fg)))
    bc = _time_cumsum(lf, reverse)
    m_prev = m_ref[0:1, :]
    a_all = bc + m_prev
    b_last = bc[0:1, :] if reverse else bc[L - 1:L, :]
    g_all = b_last - bc + ig
    m_new = jnp.maximum(b_last + m_prev, jnp.max(g_all, axis=0, keepdims=True))
    decay = jnp.exp(b_last + m_prev - m_new)
    eg = jnp.exp(g_all - m_new)
    xt = (ig - bc).T
    m_ref[...] = jnp.broadcast_to(m_new, m_ref.shape)

    li = lax.broadcasted_iota(jnp.int32, (L, L), 0)
    si = lax.broadcasted_iota(jnp.int32, (L, L), 1)
    tri = (si >= li) if reverse else (si <= li)
    qscale = dqk ** -0.5

    for h in range(H):
        q = (q_ref[:, h * dqk:(h + 1) * dqk].astype(F32) * qscale).astype(BF16)
        k = k_ref[:, h * dqk:(h + 1) * dqk]
        v = v_ref[:, h * dv:(h + 1) * dv]
        ct = c_ref[h]
        nv = n_ref[h]
        a_h = a_all[:, h:h + 1]
        dmat = jnp.where(tri, bc[:, h:h + 1] + xt[h:h + 1, :], -jnp.inf)
        m_t = jnp.maximum(a_h, jnp.max(dmat, axis=1, keepdims=True))
        qk = lax.dot_general(q, k, (((1,), (1,)), ((), ())), preferred_element_type=F32)
        w = jnp.exp(dmat - m_t) * qk
        inter = jnp.exp(a_h - m_t)
        num = inter * jnp.dot(q, ct.astype(BF16), preferred_element_type=F32) + jnp.dot(
            w.astype(BF16), v, preferred_element_type=F32)
        qn = jnp.sum(q.astype(F32) * nv, axis=1, keepdims=True)
        den = inter * qn + jnp.sum(w, axis=1, keepdims=True)
        hout = num / jnp.maximum(jnp.abs(den), jnp.exp(-m_t))

        dec_h = decay[:, h:h + 1]
        wk = eg[:, h:h + 1] * k.astype(F32)
        c_ref[h] = dec_h * ct + lax.dot_general(wk.astype(BF16), v, (((0,), (0,)), ((), ())),
                                                preferred_element_type=F32)
        n_ref[h] = dec_h * nv + jnp.sum(wk, axis=0, keepdims=True)

        if finish:
            hs = hout + hf_ref[:, h * dv:(h + 1) * dv]
            hn = _rms(hs, ng_ref[:, h * dv:(h + 1) * dv]).astype(BF16)
            og = og_ref[:, h * dv:(h + 1) * dv]
            o_ref[:, h * dv:(h + 1) * dv] = (hn * jax.nn.sigmoid(og)).astype(o_ref.dtype)
        else:
            o_ref[:, h * dv:(h + 1) * dv] = hout.astype(o_ref.dtype)


def mlstm_scan(cfg, proj, gates, bias, *, reverse, hf=None, norm_g=None):
    D, B, T, Tc, H, L = cfg.D, cfg.B, cfg.T, cfg.Tc, cfg.ml_heads, cfg.chunk
    dqk, dv = cfg.ml_qk, cfg.ml_v
    ncc, ncl = Tc // L, T // L
    nc = ncc + ncl
    finish = hf is not None
    qw = H * dqk
    assert D % qw == 0 and (H * dv) == D

    def rb(b, c):
        if reverse:
            return jnp.where(c < ncc, cfg.n_lat // L + b * ncc + (ncc - 1 - c), b * ncl + (nc - 1 - c))
        return jnp.where(c < ncc, cfg.n_lat // L + b * ncc + c, b * ncl + (c - ncc))

    in_specs = [
        pl.BlockSpec((L, qw), lambda b, c: (rb(b, c), 0)),
        pl.BlockSpec((L, qw), lambda b, c: (rb(b, c), 1)),
        pl.BlockSpec((L, D), lambda b, c: (rb(b, c), 2 * qw // D)),
        pl.BlockSpec((L, LANES), lambda b, c: (rb(b, c), 0)),
        pl.BlockSpec((1, LANES), lambda b, c: (0, 0)),
    ]
    args = [proj, proj, proj, gates, bias]
    if finish:
        in_specs += [
            pl.BlockSpec((L, D), lambda b, c: (rb(b, c), 0)),
            pl.BlockSpec((L, D), lambda b, c: (rb(b, c), 2 * qw // D + 1)),
            pl.BlockSpec((1, D), lambda b, c: (0, 0)),
        ]
        args += [hf, proj, norm_g]
    return pl.pallas_call(
        functools.partial(_mlstm_kernel, H=H, dqk=dqk, dv=dv, L=L, reverse=reverse, finish=finish,
                          gate_off=2 * H if reverse else 0),
        grid=(B, nc),
        in_specs=in_specs,
        out_specs=pl.BlockSpec((L, D), lambda b, c: (rb(b, c), 0)),
        out_shape=jax.ShapeDtypeStruct((cfg.R, D), BF16 if finish else F32),
        scratch_shapes=[pltpu.VMEM((H, dqk, dv), F32), pltpu.VMEM((H, 1, dqk), F32), pltpu.VMEM((8, LANES), F32)],
        compiler_params=_cparams(("parallel", "arbitrary")),
        name="mlstm_bwd" if reverse else "mlstm_fwd",
    )(*args)


def _moe_up_kernel(x_ref, wg_ref, wu_ref, o_ref, wgb_ref, wub_ref):
    @pl.when(pl.program_id(2) == 0)
    def _():
        wgb_ref[...] = wg_ref[...].astype(BF16)
        wub_ref[...] = wu_ref[...].astype(BF16)

    x = x_ref[...]
    a = jnp.dot(x, wgb_ref[...], preferred_element_type=F32)
    u = jnp.dot(x, wub_ref[...], preferred_element_type=F32)
    o_ref[...] = (a * jax.nn.sigmoid(a) * u).astype(o_ref.dtype)


def _row_tile(n, cap):
    best = n
    for t in range(16, min(n, cap) + 1, 16):
        if n % t == 0:
            best = t
    return best


def moe_up(cfg, xe, w_gate, w_up, layer):
    E, P, D = xe.shape
    F = cfg.F
    tp = _row_tile(P, 528)
    fk = _tile(F, 512)
    w_spec = pl.BlockSpec((None, None, D, fk), lambda e, f, p: (layer, e, 0, f))
    return pl.pallas_call(
        _moe_up_kernel,
        grid=(E, F // fk, P // tp),
        in_specs=[pl.BlockSpec((None, tp, D), lambda e, f, p: (e, p, 0)), w_spec, w_spec],
        out_specs=pl.BlockSpec((None, tp, fk), lambda e, f, p: (e, p, f)),
        out_shape=jax.ShapeDtypeStruct((E, P, F), BF16),
        scratch_shapes=[pltpu.VMEM((D, fk), BF16), pltpu.VMEM((D, fk), BF16)],
        compiler_params=_cparams(("parallel", "parallel", "arbitrary")),
        name="moe_up",
    )(xe, w_gate, w_up)


def _moe_down_kernel(rows_ref, hid_ref, g_ref, wd_ref, acc_in_ref, acc_ref, wdb_ref, buf_ref, sem_in, sem_out, *,
                     tp, P, nblk):
    del acc_in_ref
    e, p = pl.program_id(0), pl.program_id(1)
    step = e * pl.num_programs(1) + p
    last = pl.num_programs(0) * pl.num_programs(1) - 1
    slot = lax.rem(step, 2)
    base = e * P + p * tp
    blk = tp // nblk

    def tile_copy(s, sem):
        return pltpu.make_async_copy(acc_ref.at[pl.ds(0, tp)], buf_ref.at[s], sem)

    @pl.when(p == 0)
    def _():
        wdb_ref[...] = wd_ref[...].astype(BF16)

        @pl.when(step > 0)
        def _():
            tile_copy(1 - slot, sem_out.at[1 - slot]).wait()

    ys = []
    for i in range(nblk):
        r0 = i * blk
        for r in range(r0, r0 + blk):
            pltpu.make_async_copy(acc_ref.at[pl.ds(rows_ref[base + r], 1)], buf_ref.at[slot, pl.ds(r, 1)],
                                  sem_in).start()
        ys.append(jnp.dot(hid_ref[r0:r0 + blk, :], wdb_ref[...], preferred_element_type=F32)
                  * g_ref[r0:r0 + blk, :])
    tile_copy(slot, sem_in).wait()
    for i in range(nblk):
        r0 = i * blk
        buf_ref[slot, r0:r0 + blk, :] += ys[i]
        for r in range(r0, r0 + blk):
            pltpu.make_async_copy(buf_ref.at[slot, pl.ds(r, 1)], acc_ref.at[pl.ds(rows_ref[base + r], 1)],
                                  sem_out.at[slot]).start()

    @pl.when(jnp.logical_and(p != 0, step > 0))
    def _():
        tile_copy(1 - slot, sem_out.at[1 - slot]).wait()

    @pl.when(step == last)
    def _():
        tile_copy(slot, sem_out.at[slot]).wait()


def moe_down_scatter(cfg, rows, hid, ge, w_down, layer, n_rows):
    E, P, F = hid.shape
    D = cfg.D
    tp = _row_tile(P, 352)
    grid_spec = pltpu.PrefetchScalarGridSpec(
        num_scalar_prefetch=1,
        grid=(E, P // tp),
        in_specs=[
            pl.BlockSpec((None, tp, F), lambda e, p, rows: (e, p, 0)),
            pl.BlockSpec((None, tp, 1), lambda e, p, rows: (e, p, 0)),
            pl.BlockSpec((None, None, F, D), lambda e, p, rows: (layer, e, 0, 0), pipeline_mode=pl.Buffered(1)),
            pl.BlockSpec(memory_space=pl.ANY),
        ],
        out_specs=pl.BlockSpec(memory_space=pl.ANY),
        scratch_shapes=[pltpu.VMEM((F, D), BF16), pltpu.VMEM((2, tp, D), F32), pltpu.SemaphoreType.DMA(()),
                        pltpu.SemaphoreType.DMA((2,))],
    )
    nblk = 2 if tp % 32 == 0 else 1
    return pl.pallas_call(
        functools.partial(_moe_down_kernel, tp=tp, P=P, nblk=nblk),
        grid_spec=grid_spec,
        out_shape=jax.ShapeDtypeStruct((n_rows, D), F32),
        input_output_aliases={4: 0},
        compiler_params=_cparams(("arbitrary", "arbitrary")),
        name="moe_down_scatter",
    )(rows.reshape(-1), hid, ge, w_down, jnp.zeros((n_rows, D), F32))


def expert_choice_moe(cfg, h, logits, w_gate, w_up, w_down, layer, with_ctx):
    B, T, Tc, E = cfg.B, cfg.T, cfg.Tc, cfg.E
    n_rows = cfg.R if with_ctx else cfg.n_lat
    aff = jax.nn.softmax(logits[:n_rows, :E], axis=-1)
    sets = [(b * T, T) for b in range(B)]
    if with_ctx:
        sets += [(cfg.n_lat + b * Tc, Tc) for b in range(B)]
    rows, gates = [], []
    for start, n in sets:
        cap = max(1, (cfg.cap_factor * n) // E)
        g, idx = lax.top_k(aff[start:start + n].T, cap)
        rows.append(idx + start)
        gates.append(g)
    rows = jnp.concatenate(rows, axis=1)
    gates = jnp.concatenate(gates, axis=1)
    xe = h[rows]
    hid = moe_up(cfg, xe, w_gate, w_up, layer)
    return moe_down_scatter(cfg, rows, hid, gates[..., None], w_down, layer, n_rows)


def _forward(cfg, x, c, ctx, c_ctx, ada_w, ada_b, norm_g, attn_w_in, attn_w_out, attn_lambda, attn_subln_g,
             mlstm_w_in, mlstm_b_gates, mlstm_norm_g, mlstm_w_out, router_w, expert_w_gate, expert_w_up,
             expert_w_down):
    D, B, T, Tc, E = cfg.D, cfg.B, cfg.T, cfg.Tc, cfg.E
    depth = ada_w.shape[0]
    cos_t, sin_t = rope_tables(cfg)
    cvec = jnp.concatenate([c, c_ctx[None, :], jnp.zeros((8 - B - 1, D), F32)], axis=0)
    ada_b3 = ada_b.reshape(depth, 1, 6 * D)
    norm_g3 = norm_g.reshape(depth * 4, 1, D)
    X = (x.reshape(B * T, D), ctx.reshape(B * Tc, D))

    mods = [adaln(cvec, ada_w, ada_b3, i)[:B + 1].reshape((B + 1) * 6, 1, D) for i in range(depth)]
    rw = jnp.pad(router_w, ((0, 0), (0, 0), (0, LANES - E)))
    rw_hi = rw.astype(BF16)
    rw_lo = (rw - rw_hi.astype(F32)).astype(BF16)

    _, h, _ = resid_norm(cfg, X, norm_g3, h_part=(mods[0], 0, 0, 1, 0))
    for i in range(depth):
        last = i == depth - 1
        mod = mods[i]
        j = i // 2
        n_rows = cfg.n_lat if last else cfg.R
        if i % 2 == 0:
            lam_init = 0.8 - 0.6 * math.exp(-0.3 * i)
            w_in = attn_w_in[j]
            w_qk = rope_col_order(cfg, w_in[:, :2 * D]).astype(BF16)
            qk_lat = qk_proj_rope(cfg, h, w_qk, cos_t, sin_t)
            qk_ctx = matmul(h, w_qk, BF16, row0=cfg.n_lat, scale_cols=D,
                            scale=cfg.da_dim ** -0.5 * math.log2(math.e))
            v = matmul_wcast(h, attn_w_in, j, BF16, col0=2 * D, n_cols=D)
            g2 = attn_subln_g[j].reshape(1, -1)
            o = diff_attention(cfg, qk_lat, qk_ctx, v, attn_lambda[j], g2, lam_init, latent=True)
            if not last:
                o = (o, diff_attention(cfg, qk_lat, qk_ctx, v, attn_lambda[j], g2, lam_init, latent=False))
            y = matmul_wcast(o, attn_w_out, j, BF16)
        else:
            H = cfg.ml_heads
            nmain = mlstm_w_in.shape[2] - 4 * H
            proj = matmul_wcast(h, mlstm_w_in, j, BF16, n_cols=nmain)
            wg = jnp.pad(mlstm_w_in[j, :, nmain:], ((0, 0), (0, LANES - 4 * H))).astype(BF16)
            gates = matmul(h, wg, F32)
            bias = jnp.pad(mlstm_b_gates[j].reshape(1, 4 * H), ((0, 0), (0, LANES - 4 * H)))
            hf = mlstm_scan(cfg, proj, gates, bias, reverse=False)
            hn = mlstm_scan(cfg, proj, gates, bias, reverse=True, hf=hf, norm_g=mlstm_norm_g[j].reshape(1, D))
            y = matmul_wcast(hn, mlstm_w_out, j, BF16, n_rows=n_rows)
        X, h, lg = resid_norm(cfg, X, norm_g3, y_part=(y, mod, i, 2, 1), h_part=(mod, i, 2, 4, 3),
                              router=(rw_hi[i], rw_lo[i]), n_rows=n_rows)
        mo = expert_choice_moe(cfg, h, lg, expert_w_gate, expert_w_up, expert_w_down, i, with_ctx=not last)
        if last:
            X, _, _ = resid_norm(cfg, X, norm_g3, y_part=(mo, mod, i, 5, 3), n_rows=n_rows)
        else:
            X, h, _ = resid_norm(cfg, X, norm_g3, y_part=(mo, mod, i, 5, 3), h_part=(mods[i + 1], i + 1, 0, 1, 0))
    return X.reshape(B, T, D)


def kernel(x, c, ctx, c_ctx, ada_w, ada_b, norm_g, attn_w_in, attn_w_out, attn_lambda, attn_subln_g, mlstm_w_in,
           mlstm_b_gates, mlstm_norm_g, mlstm_w_out, router_w, expert_w_gate, expert_w_up, expert_w_down):
    return _forward(FULL_CFG, x, c, ctx, c_ctx, ada_w, ada_b, norm_g, attn_w_in, attn_w_out, attn_lambda,
                    attn_subln_g, mlstm_w_in, mlstm_b_gates, mlstm_norm_g, mlstm_w_out, router_w, expert_w_gate,
                    expert_w_up, expert_w_down)
```

```python
import functools
import math
from typing import NamedTuple

import jax
import jax.numpy as jnp
from jax import lax
from jax.experimental import pallas as pl
from jax.experimental.pallas import tpu as pltpu

BF16 = jnp.bfloat16
F32 = jnp.float32

EPS = 1e-6
GATE_CAP = 15.0
ROPE_BASE = 10000.0
LANES = 128
MIB = 1024 * 1024
VMEM_LIMIT = 56 * MIB

class Cfg(NamedTuple):
    D: int
    B: int
    T: int
    Tc: int
    grid_w: int
    da_heads: int
    da_dim: int
    ml_heads: int
    E: int
    F: int
    chunk: int
    cap_factor: int

    @property
    def n_lat(self):
        return self.B * self.T

    @property
    def n_ctx(self):
        return self.B * self.Tc

    @property
    def R(self):
        return self.n_lat + self.n_ctx

    @property
    def ml_qk(self):
        return self.D // 2 // self.ml_heads

    @property
    def ml_v(self):
        return self.D // self.ml_heads


FULL_CFG = Cfg(D=4096, B=2, T=8192, Tc=256, grid_w=64, da_heads=16, da_dim=128, ml_heads=8, E=16, F=1024,
               chunk=256, cap_factor=2)


def _cparams(sem, vmem=VMEM_LIMIT):
    return pltpu.CompilerParams(dimension_semantics=sem, vmem_limit_bytes=vmem)


def _tile(n, pref):
    t = min(n, pref)
    while n % t:
        t //= 2
    return t


def _lane_tile(x, reps):
    return x if reps == 1 else jnp.concatenate([x] * reps, axis=1)


def _adaln_kernel(c_ref, w_ref, b_ref, o_ref):
    c = c_ref[...]
    s = (c * jax.nn.sigmoid(c)).astype(BF16)
    o_ref[...] = jnp.dot(s, w_ref[...].astype(BF16), preferred_element_type=F32) + b_ref[...]


def adaln(cvec, ada_w, ada_b3, layer):
    D = cvec.shape[1]
    N = ada_w.shape[2]
    tn = _tile(N, 512)
    return pl.pallas_call(
        _adaln_kernel,
        grid=(N // tn,),
        in_specs=[
            pl.BlockSpec((8, D), lambda j: (0, 0)),
            pl.BlockSpec((None, D, tn), lambda j: (layer, 0, j)),
            pl.BlockSpec((None, 1, tn), lambda j: (layer, 0, j)),
        ],
        out_specs=pl.BlockSpec((8, tn), lambda j: (0, j)),
        out_shape=jax.ShapeDtypeStruct((8, N), F32),
        compiler_params=_cparams(("parallel",)),
        name="adaln",
    )(cvec, ada_w, ada_b3)


def _rms(x, g):
    return x * lax.rsqrt(jnp.mean(x * x, axis=-1, keepdims=True) + EPS) * g


def _resid_norm_kernel(*refs, split_at, has_y, want_h, want_logits):
    it = iter(refs)
    x_ref = next(it)
    xc_ref = next(it) if split_at else None
    if has_y:
        y_ref, gate_ref, gpost_ref = next(it), next(it), next(it)
    if want_h:
        gpre_ref, sc_ref, sh_ref = next(it), next(it), next(it)
    if want_logits:
        rwh_ref, rwl_ref = next(it), next(it)
    if has_y:
        xo_ref = next(it)
    if want_h:
        h_ref = next(it)
    if want_logits:
        lg_ref = next(it)

    x = x_ref[...]
    if split_at:
        x = jnp.where(pl.program_id(0) < split_at, x, xc_ref[...])
    if has_y:
        y = y_ref[...].astype(F32)
        x = x + gate_ref[0] * _rms(y, gpost_ref[0])
        xo_ref[...] = x
    if want_h:
        h = _rms(x, gpre_ref[0]) * (1.0 + sc_ref[0]) + sh_ref[0]
        hb = h.astype(BF16)
        h_ref[...] = hb
        if want_logits:
            hl = (h - hb.astype(F32)).astype(BF16)
            lg_ref[...] = (jnp.dot(hb, rwh_ref[...], preferred_element_type=F32)
                           + jnp.dot(hl, rwh_ref[...], preferred_element_type=F32)
                           + jnp.dot(hb, rwl_ref[...], preferred_element_type=F32))


def resid_norm(cfg, x, norm_g, *, y_part=None, h_part=None, router=None, n_rows=None):
    D = cfg.D
    tm = _tile(cfg.Tc, 256)
    count = n_rows if n_rows is not None else cfg.R
    assert count % tm == 0
    nlat_t = cfg.n_lat // tm
    tpb = cfg.T // tm
    has_y, want_h, want_logits = y_part is not None, h_part is not None, router is not None

    def set_of(i):
        return jnp.where(i < nlat_t, i // tpb, cfg.B)

    row_spec = lambda w: pl.BlockSpec((tm, w), lambda i: (i, 0))
    mod_spec = lambda k: pl.BlockSpec((1, 1, D), lambda i: (set_of(i) * 6 + k, 0, 0))
    g_spec = lambda layer, k: pl.BlockSpec((1, 1, D), lambda i: (layer * 4 + k, 0, 0))

    split = isinstance(x, tuple)
    if split:
        args = list(x)
        in_specs = [pl.BlockSpec((tm, D), lambda i: (jnp.minimum(i, nlat_t - 1), 0)),
                    pl.BlockSpec((tm, D), lambda i: (jnp.maximum(i - nlat_t, 0), 0))]
    else:
        args, in_specs = [x], [row_spec(D)]
    out_shape, out_specs = [], []
    if has_y:
        y, mod, layer, gate_k, gpost_k = y_part
        args += [y, mod, norm_g]
        in_specs += [row_spec(D), mod_spec(gate_k), g_spec(layer, gpost_k)]
    if want_h:
        mod, layer, gpre_k, sc_k, sh_k = h_part
        args += [norm_g, mod, mod]
        in_specs += [g_spec(layer, gpre_k), mod_spec(sc_k), mod_spec(sh_k)]
    if want_logits:
        args += [router[0], router[1]]
        in_specs += [pl.BlockSpec((D, LANES), lambda i: (0, 0))] * 2
    if has_y:
        out_shape.append(jax.ShapeDtypeStruct((count, D), F32))
        out_specs.append(row_spec(D))
    if want_h:
        out_shape.append(jax.ShapeDtypeStruct((count, D), BF16))
        out_specs.append(row_spec(D))
    if want_logits:
        out_shape.append(jax.ShapeDtypeStruct((count, LANES), F32))
        out_specs.append(row_spec(LANES))

    outs = pl.pallas_call(
        functools.partial(_resid_norm_kernel, split_at=nlat_t if split else 0, has_y=has_y, want_h=want_h,
                          want_logits=want_logits),
        grid=(count // tm,),
        in_specs=in_specs,
        out_specs=out_specs,
        out_shape=out_shape,
        compiler_params=_cparams(("parallel",)),
        name="resid_norm",
    )(*args)
    outs = list(outs)
    x_new = outs.pop(0) if has_y else None
    h = outs.pop(0) if want_h else None
    lg = outs.pop(0) if want_logits else None
    return x_new, h, lg


def _mm_kernel(x_ref, w_ref, o_ref, *, scale_tiles, scale):
    acc = jnp.dot(x_ref[...], w_ref[...], preferred_element_type=F32)
    if scale_tiles:
        acc = acc * jnp.where(pl.program_id(1) < scale_tiles, scale, 1.0).astype(F32)
    o_ref[...] = acc.astype(o_ref.dtype)


def matmul(x, w, out_dtype, n_rows=None, row0=0, scale_cols=0, scale=1.0, tm_pref=1024, tn_pref=1024):
    K = x.shape[1]
    M = n_rows if n_rows is not None else x.shape[0] - row0
    N = w.shape[1]
    tm, tn = _tile(math.gcd(M, row0), tm_pref), _tile(math.gcd(N, scale_cols), tn_pref)
    assert row0 % tm == 0 and scale_cols % tn == 0
    r0 = row0 // tm
    return pl.pallas_call(
        functools.partial(_mm_kernel, scale_tiles=scale_cols // tn, scale=scale),
        grid=(M // tm, N // tn),
        in_specs=[pl.BlockSpec((tm, K), lambda i, j: (i + r0, 0)), pl.BlockSpec((K, tn), lambda i, j: (0, j))],
        out_specs=pl.BlockSpec((tm, tn), lambda i, j: (i, j)),
        out_shape=jax.ShapeDtypeStruct((M, N), out_dtype),
        compiler_params=_cparams(("parallel", "arbitrary")),
        name="matmul",
    )(x, w)


def _mm_wcast_kernel(*refs, na_tiles):
    if na_tiles:
        xa_ref, xb_ref, w_ref, o_ref, wb_ref = refs
    else:
        xa_ref, w_ref, o_ref, wb_ref = refs

    @pl.when(pl.program_id(1) == 0)
    def _():
        wb_ref[...] = w_ref[...].astype(BF16)

    x = xa_ref[...]
    if na_tiles:
        x = jnp.where(pl.program_id(1) < na_tiles, x, xb_ref[...])
    o_ref[...] = jnp.dot(x, wb_ref[...], preferred_element_type=F32).astype(o_ref.dtype)


def matmul_wcast(x, w, lead, out_dtype, n_rows=None, col0=0, n_cols=None, tm_pref=512, tn_pref=1024):
    xs = x if isinstance(x, tuple) else (x,)
    K = xs[0].shape[1]
    N = n_cols if n_cols is not None else w.shape[2]
    if len(xs) == 2:
        Ma, Mb = xs[0].shape[0], xs[1].shape[0]
        M = Ma + Mb
        tm = _tile(math.gcd(Ma, Mb), tm_pref)
        na = Ma // tm
        x_specs = [pl.BlockSpec((tm, K), lambda j, i: (jnp.minimum(i, na - 1), 0)),
                   pl.BlockSpec((tm, K), lambda j, i: (jnp.maximum(i - na, 0), 0))]
    else:
        M = n_rows if n_rows is not None else xs[0].shape[0]
        tm, na = _tile(M, tm_pref), 0
        x_specs = [pl.BlockSpec((tm, K), lambda j, i: (i, 0))]
    tn = _tile(math.gcd(N, col0), tn_pref)
    c0 = col0 // tn
    return pl.pallas_call(
        functools.partial(_mm_wcast_kernel, na_tiles=na),
        grid=(N // tn, M // tm),
        in_specs=x_specs + [pl.BlockSpec((None, K, tn), lambda j, i: (lead, 0, j + c0),
                                         pipeline_mode=pl.Buffered(1))],
        out_specs=pl.BlockSpec((tm, tn), lambda j, i: (i, j)),
        out_shape=jax.ShapeDtypeStruct((M, N), out_dtype),
        scratch_shapes=[pltpu.VMEM((K, tn), BF16)],
        compiler_params=_cparams(("parallel", "arbitrary")),
        name="matmul_wcast",
    )(*xs, w)


def _qk_rope_kernel(x_ref, w_ref, cos_ref, sin_ref, o_ref, *, q_tiles, qscale):
    scale = jnp.where(pl.program_id(1) < q_tiles, qscale, 1.0).astype(F32)
    cos, sin = cos_ref[...] * scale, sin_ref[...] * scale
    x = x_ref[...]
    tn = o_ref.shape[1]
    half = tn // 2 if (tn // 2) % LANES == 0 else tn
    for c0 in range(0, tn, half):
        acc = jnp.dot(x, w_ref[:, c0:c0 + half], preferred_element_type=F32)
        for g in range(half // LANES):
            xg = acc[:, g * LANES:(g + 1) * LANES]
            o_ref[:, c0 + g * LANES:c0 + (g + 1) * LANES] = (
                xg * cos + pltpu.roll(xg, LANES // 2, 1) * sin).astype(o_ref.dtype)


def qk_proj_rope(cfg, h, w_qk, cos_t, sin_t):
    D = cfg.D
    N = w_qk.shape[1]
    tm = _tile(cfg.T, 512)
    tn = _tile(D, 512)
    tpb = cfg.T // tm
    qscale = cfg.da_dim ** -0.5 * math.log2(math.e)
    tbl = pl.BlockSpec((tm, LANES), lambda i, j: (i % tpb, 0))
    return pl.pallas_call(
        functools.partial(_qk_rope_kernel, q_tiles=D // tn, qscale=qscale),
        grid=(cfg.n_lat // tm, N // tn),
        in_specs=[pl.BlockSpec((tm, D), lambda i, j: (i, 0)), pl.BlockSpec((D, tn), lambda i, j: (0, j)), tbl, tbl],
        out_specs=pl.BlockSpec((tm, tn), lambda i, j: (i, j)),
        out_shape=jax.ShapeDtypeStruct((cfg.n_lat, N), BF16),
        compiler_params=_cparams(("parallel", "arbitrary")),
        name="qk_proj_rope",
    )(h, w_qk, cos_t, sin_t)


def rope_col_order(cfg, w):
    K, N = w.shape
    q = cfg.da_dim // 4
    return w.reshape(K, N // cfg.da_dim, 2, 2, q).transpose(0, 1, 3, 2, 4).reshape(K, N)


def rope_tables(cfg):
    half = cfg.da_dim // 2
    t = jnp.arange(cfg.T)
    row = (t // cfg.grid_w).astype(F32)
    col = (t % cfg.grid_w).astype(F32)
    inv = ROPE_BASE ** (-jnp.arange(0, half, 2, dtype=F32) / half)
    ar, ac = row[:, None] * inv, col[:, None] * inv
    cos_t = jnp.concatenate([jnp.cos(ar), jnp.cos(ac), jnp.cos(ar), jnp.cos(ac)], axis=1)
    sin_t = jnp.concatenate([-jnp.sin(ar), -jnp.sin(ac), jnp.sin(ar), jnp.sin(ac)], axis=1)
    return cos_t, sin_t


def _attn_kernel(*refs, tk, n_lat, lam_init, dd):
    if n_lat:
        (lam_ref, g_ref, q_ref, kc_ref, vc_ref, kl_ref, vl_ref, o_ref, acc_ref, m_ref, l_ref, sa_ref,
         sb_ref) = refs
    else:
        lam_ref, g_ref, q_ref, kc_ref, vc_ref, o_ref, acc_ref, m_ref, l_ref = refs
    d = dd // 2
    q = q_ref[...]

    m_ref[...] = jnp.full(m_ref.shape, -jnp.inf, F32)
    l_ref[...] = jnp.zeros(l_ref.shape, F32)
    acc_ref[...] = jnp.zeros(acc_ref.shape, F32)

    tq = q.shape[0]
    every = slice(0, tq)

    def score(k, p, rows):
        return lax.dot_general(q[rows, p * d:(p + 1) * d], k[:, p * d:(p + 1) * d], (((1,), (1,)), ((), ())),
                               preferred_element_type=F32)

    def softmax_pv(s_of, v, rows):
        n = v.shape[0]
        nh = n // 2 if (n // 2) % LANES == 0 else n
        for p in range(2):
            m_prev = m_ref[p, rows]
            m_next = jnp.maximum(m_prev, jnp.max(s_of(p, slice(0, n)), axis=1)[:, None])
            alpha = jnp.exp2(m_prev - m_next)
            m_rep = _lane_tile(m_next, nh // LANES)
            l_new = alpha * l_ref[p, rows]
            acc = acc_ref[p, rows] * _lane_tile(alpha, dd // LANES)
            for c0 in range(0, n, nh):
                pr = jnp.exp2(s_of(p, slice(c0, c0 + nh)) - m_rep)
                l_new = l_new + jnp.sum(pr, axis=1)[:, None]
                acc = acc + jnp.dot(pr.astype(BF16), v[c0:c0 + nh, :], preferred_element_type=F32)
            l_ref[p, rows] = l_new
            m_ref[p, rows] = m_next
            acc_ref[p, rows] = acc

    kc = kc_ref[...]
    sc = [score(kc, p, every) for p in range(2)]
    if not n_lat:
        softmax_pv(lambda p, cols: sc[p][:, cols], vc_ref[...], every)
    else:
        half_a, half_b = slice(0, tq // 2), slice(tq // 2, tq)

        def chunk(ref, j):
            return ref[pl.ds(pl.multiple_of(j * tk, tk), tk), :]

        def scores_to(s_ref, j, rows):
            k = chunk(kl_ref, j)
            for p in range(2):
                s_ref[p] = score(k, p, rows)

        from_a = lambda p, cols: sa_ref[p, :, cols]
        from_b = lambda p, cols: sb_ref[p, :, cols]
        scores_to(sa_ref, 0, half_a)
        softmax_pv(lambda p, cols: sc[p][:, cols], vc_ref[...], every)

        def body(j, carry):
            scores_to(sb_ref, j, half_b)
            softmax_pv(from_a, chunk(vl_ref, j), half_a)
            scores_to(sa_ref, j + 1, half_a)
            softmax_pv(from_b, chunk(vl_ref, j), half_b)
            return carry

        lax.fori_loop(0, n_lat - 1, body, 0)
        scores_to(sb_ref, n_lat - 1, half_b)
        softmax_pv(from_a, chunk(vl_ref, n_lat - 1), half_a)
        softmax_pv(from_b, chunk(vl_ref, n_lat - 1), half_b)

    lv = lam_ref[...]
    lam = (jnp.exp(jnp.sum(lv[0:1] * lv[1:2], axis=1, keepdims=True))
           - jnp.exp(jnp.sum(lv[2:3] * lv[3:4], axis=1, keepdims=True)) + lam_init)
    o1 = acc_ref[0] / _lane_tile(l_ref[0], dd // LANES)
    o2 = acc_ref[1] / _lane_tile(l_ref[1], dd // LANES)
    o = o1 - lam * o2
    o = _rms(o, g_ref[...]) * (1.0 - lam_init)
    o_ref[...] = o.astype(o_ref.dtype)


def diff_attention(cfg, qk_lat, qk_ctx, v, lam_vecs, subln_g, lam_init, *, latent):
    D, B, T, Tc, H = cfg.D, cfg.B, cfg.T, cfg.Tc, cfg.da_heads
    dd = 2 * cfg.da_dim
    assert T % Tc == 0 and dd % LANES == 0 and Tc % LANES == 0
    ctx_blk0 = cfg.n_lat // Tc
    small = lambda shape: pl.BlockSpec(shape, lambda b, h, i: (0, 0))
    kv_ctx = [pl.BlockSpec((Tc, dd), lambda b, h, i: (b, H + h)),
              pl.BlockSpec((Tc, dd), lambda b, h, i: (ctx_blk0 + b, h))]
    if latent:
        tq, tk = _tile(T, 1024), _tile(T // 2, 1024)
        nq, n_lat = T // tq, T // tk
        assert (tq // 2) % 16 == 0
        in_specs = [small((4, cfg.da_dim)), small((1, dd)),
                    pl.BlockSpec((tq, dd), lambda b, h, i: (b * nq + i, h))] + kv_ctx + [
            pl.BlockSpec((T, dd), lambda b, h, i: (b, H + h)),
            pl.BlockSpec((T, dd), lambda b, h, i: (b, h))]
        args = [lam_vecs, subln_g, qk_lat, qk_ctx, v, qk_lat, v]
        extra = [pltpu.VMEM((2, tq // 2, tk), F32), pltpu.VMEM((2, tq // 2, tk), F32)]
    else:
        tq, tk, nq, n_lat = Tc, 0, 1, 0
        in_specs = [small((4, cfg.da_dim)), small((1, dd)),
                    pl.BlockSpec((tq, dd), lambda b, h, i: (b, h))] + kv_ctx
        args = [lam_vecs, subln_g, qk_ctx, qk_ctx, v]
        extra = []
    scratch = [pltpu.VMEM((2, tq, dd), F32), pltpu.VMEM((2, tq, LANES), F32), pltpu.VMEM((2, tq, LANES), F32)]
    return pl.pallas_call(
        functools.partial(_attn_kernel, tk=tk, n_lat=n_lat, lam_init=lam_init, dd=dd),
        grid=(B, H, nq),
        in_specs=in_specs,
        out_specs=pl.BlockSpec((tq, dd), lambda b, h, i: (b * nq + i, h)),
        out_shape=jax.ShapeDtypeStruct((B * (T if latent else Tc), D), BF16),
        scratch_shapes=scratch + extra,
        compiler_params=_cparams(("parallel", "parallel", "arbitrary")),
        name="diff_attn_lat" if latent else "diff_attn_ctx",
    )(*args)


def _time_cumsum(x, reverse):
    L = x.shape[0]
    row = lax.broadcasted_iota(jnp.int32, x.shape, 0)
    sh = 1
    while sh < L:
        if reverse:
            x = x + jnp.where(row < L - sh, pltpu.roll(x, L - sh, 0), 0.0)
        else:
            x = x + jnp.where(row >= sh, pltpu.roll(x, sh, 0), 0.0)
        sh *= 2
    return x


def _mlstm_kernel(*refs, H, dqk, dv, L, reverse, finish, gate_off):
    if finish:
        (q_ref, k_ref, v_ref, gt_ref, bias_ref, hf_ref, og_ref, ng_ref, o_ref, c_ref, n_ref, m_ref) = refs
    else:
        (q_ref, k_ref, v_ref, gt_ref, bias_ref, o_ref, c_ref, n_ref, m_ref) = refs

    @pl.when(pl.program_id(1) == 0)
    def _():
        c_ref[...] = jnp.zeros(c_ref.shape, F32)
        n_ref[...] = jnp.zeros(n_ref.shape, F32)
        m_ref[...] = jnp.zeros(m_ref.shape, F32)

    gates = GATE_CAP * jnp.tanh((gt_ref[...] + bias_ref[...]) * (1.0 / GATE_CAP))
    ig = pltpu.roll(gates, LANES - gate_off, 1) if gate_off else gates
    fg = pltpu.roll(gates, LANES - gate_off - H, 1)
    lf = jnp.minimum(fg, 0.0) - jnp.log1p(jnp.exp(-jnp.abs(fg)))
    bc = _time_cumsum(lf, reverse)
    m_prev = m_ref[0:1, :]
    a_all = bc + m_prev
    b_last = bc[0:1, :] if reverse else bc[L - 1:L, :]
    g_all = b_last - bc + ig
    m_new = jnp.maximum(b_last + m_prev, jnp.max(g_all, axis=0, keepdims=True))
    decay = jnp.exp(b_last + m_prev - m_new)
    eg = jnp.exp(g_all - m_new)
    xt = (ig - bc).T
    m_ref[...] = jnp.broadcast_to(m_new, m_ref.shape)

    li = lax.broadcasted_iota(jnp.int32, (L, L), 0)
    si = lax.broadcasted_iota(jnp.int32, (L, L), 1)
    tri = (si >= li) if reverse else (si <= li)
    qscale = dqk ** -0.5

    for h in range(H):
        q = (q_ref[:, h * dqk:(h + 1) * dqk].astype(F32) * qscale).astype(BF16)
        k = k_ref[:, h * dqk:(h + 1) * dqk]
        v = v_ref[:, h * dv:(h + 1) * dv]
        ct = c_ref[h]
        nv = n_ref[h]
        a_h = a_all[:, h:h + 1]
        dmat = jnp.where(tri, bc[:, h:h + 1] + xt[h:h + 1, :], -jnp.inf)
        m_t = jnp.maximum(a_h, jnp.max(dmat, axis=1, keepdims=True))
        qk = lax.dot_general(q, k, (((1,), (1,)), ((), ())), preferred_element_type=F32)
        w = jnp.exp(dmat - m_t) * qk
        inter = jnp.exp(a_h - m_t)
        num = inter * jnp.dot(q, ct.astype(BF16), preferred_element_type=F32) + jnp.dot(
            w.astype(BF16), v, preferred_element_type=F32)
        qn = jnp.sum(q.astype(F32) * nv, axis=1, keepdims=True)
        den = inter * qn + jnp.sum(w, axis=1, keepdims=True)
        hout = num / jnp.maximum(jnp.abs(den), jnp.exp(-m_t))

        dec_h = decay[:, h:h + 1]
        wk = eg[:, h:h + 1] * k.astype(F32)
        c_ref[h] = dec_h * ct + lax.dot_general(wk.astype(BF16), v, (((0,), (0,)), ((), ())),
                                                preferred_element_type=F32)
        n_ref[h] = dec_h * nv + jnp.sum(wk, axis=0, keepdims=True)

        if finish:
            hs = hout + hf_ref[:, h * dv:(h + 1) * dv]
            hn = _rms(hs, ng_ref[:, h * dv:(h + 1) * dv]).astype(BF16)
            og = og_ref[:, h * dv:(h + 1) * dv]
            o_ref[:, h * dv:(h + 1) * dv] = (hn * jax.nn.sigmoid(og)).astype(o_ref.dtype)
        else:
            o_ref[:, h * dv:(h + 1) * dv] = hout.astype(o_ref.dtype)


def mlstm_scan(cfg, proj, gates, bias, *, reverse, hf=None, norm_g=None):
    D, B, T, Tc, H, L = cfg.D, cfg.B, cfg.T, cfg.Tc, cfg.ml_heads, cfg.chunk
    dqk, dv = cfg.ml_qk, cfg.ml_v
    ncc, ncl = Tc // L, T // L
    nc = ncc + ncl
    finish = hf is not None
    qw = H * dqk
    assert D % qw == 0 and (H * dv) == D

    def rb(b, c):
        if reverse:
            return jnp.where(c < ncc, cfg.n_lat // L + b * ncc + (ncc - 1 - c), b * ncl + (nc - 1 - c))
        return jnp.where(c < ncc, cfg.n_lat // L + b * ncc + c, b * ncl + (c - ncc))

    in_specs = [
        pl.BlockSpec((L, qw), lambda b, c: (rb(b, c), 0)),
        pl.BlockSpec((L, qw), lambda b, c: (rb(b, c), 1)),
        pl.BlockSpec((L, D), lambda b, c: (rb(b, c), 2 * qw // D)),
        pl.BlockSpec((L, LANES), lambda b, c: (rb(b, c), 0)),
        pl.BlockSpec((1, LANES), lambda b, c: (0, 0)),
    ]
    args = [proj, proj, proj, gates, bias]
    if finish:
        in_specs += [
            pl.BlockSpec((L, D), lambda b, c: (rb(b, c), 0)),
            pl.BlockSpec((L, D), lambda b, c: (rb(b, c), 2 * qw // D + 1)),
            pl.BlockSpec((1, D), lambda b, c: (0, 0)),
        ]
        args += [hf, proj, norm_g]
    return pl.pallas_call(
        functools.partial(_mlstm_kernel, H=H, dqk=dqk, dv=dv, L=L, reverse=reverse, finish=finish,
                          gate_off=2 * H if reverse else 0),
        grid=(B, nc),
        in_specs=in_specs,
        out_specs=pl.BlockSpec((L, D), lambda b, c: (rb(b, c), 0)),
        out_shape=jax.ShapeDtypeStruct((cfg.R, D), BF16 if finish else F32),
        scratch_shapes=[pltpu.VMEM((H, dqk, dv), F32), pltpu.VMEM((H, 1, dqk), F32), pltpu.VMEM((8, LANES), F32)],
        compiler_params=_cparams(("parallel", "arbitrary")),
        name="mlstm_bwd" if reverse else "mlstm_fwd",
    )(*args)


def _moe_up_kernel(x_ref, wg_ref, wu_ref, o_ref, wgb_ref, wub_ref):
    @pl.when(pl.program_id(2) == 0)
    def _():
        wgb_ref[...] = wg_ref[...].astype(BF16)
        wub_ref[...] = wu_ref[...].astype(BF16)

    x = x_ref[...]
    a = jnp.dot(x, wgb_ref[...], preferred_element_type=F32)
    u = jnp.dot(x, wub_ref[...], preferred_element_type=F32)
    o_ref[...] = (a * jax.nn.sigmoid(a) * u).astype(o_ref.dtype)


def _row_tile(n, cap):
    best = n
    for t in range(16, min(n, cap) + 1, 16):
        if n % t == 0:
            best = t
    return best


def moe_up(cfg, xe, w_gate, w_up, layer):
    E, P, D = xe.shape
    F = cfg.F
    tp = _row_tile(P, 528)
    fk = _tile(F, 512)
    w_spec = pl.BlockSpec((None, None, D, fk), lambda e, f, p: (layer, e, 0, f))
    return pl.pallas_call(
        _moe_up_kernel,
        grid=(E, F // fk, P // tp),
        in_specs=[pl.BlockSpec((None, tp, D), lambda e, f, p: (e, p, 0)), w_spec, w_spec],
        out_specs=pl.BlockSpec((None, tp, fk), lambda e, f, p: (e, p, f)),
        out_shape=jax.ShapeDtypeStruct((E, P, F), BF16),
        scratch_shapes=[pltpu.VMEM((D, fk), BF16), pltpu.VMEM((D, fk), BF16)],
        compiler_params=_cparams(("parallel", "parallel", "arbitrary")),
        name="moe_up",
    )(xe, w_gate, w_up)


def _moe_down_kernel(rows_ref, hid_ref, g_ref, wd_ref, acc_in_ref, acc_ref, wdb_ref, buf_ref, sem_in, sem_out, *,
                     tp, P, nblk):
    del acc_in_ref
    e, p = pl.program_id(0), pl.program_id(1)
    step = e * pl.num_programs(1) + p
    last = pl.num_programs(0) * pl.num_programs(1) - 1
    slot = lax.rem(step, 2)
    base = e * P + p * tp
    blk = tp // nblk

    def tile_copy(s, sem):
        return pltpu.make_async_copy(acc_ref.at[pl.ds(0, tp)], buf_ref.at[s], sem)

    @pl.when(p == 0)
    def _():
        wdb_ref[...] = wd_ref[...].astype(BF16)

        @pl.when(step > 0)
        def _():
            tile_copy(1 - slot, sem_out.at[1 - slot]).wait()

    ys = []
    for i in range(nblk):
        r0 = i * blk
        for r in range(r0, r0 + blk):
            pltpu.make_async_copy(acc_ref.at[pl.ds(rows_ref[base + r], 1)], buf_ref.at[slot, pl.ds(r, 1)],
                                  sem_in).start(priority=r % 2)
        ys.append(jnp.dot(hid_ref[r0:r0 + blk, :], wdb_ref[...], preferred_element_type=F32)
                  * g_ref[r0:r0 + blk, :])
    tile_copy(slot, sem_in).wait()
    for i in range(nblk):
        r0 = i * blk
        buf_ref[slot, r0:r0 + blk, :] += ys[i]
        for r in range(r0, r0 + blk):
            pltpu.make_async_copy(buf_ref.at[slot, pl.ds(r, 1)], acc_ref.at[pl.ds(rows_ref[base + r], 1)],
                                  sem_out.at[slot]).start(priority=r % 2)

    @pl.when(jnp.logical_and(p != 0, step > 0))
    def _():
        tile_copy(1 - slot, sem_out.at[1 - slot]).wait()

    @pl.when(step == last)
    def _():
        tile_copy(slot, sem_out.at[slot]).wait()


def moe_down_scatter(cfg, rows, hid, ge, w_down, layer, n_rows):
    E, P, F = hid.shape
    D = cfg.D
    tp = _row_tile(P, 352)
    grid_spec = pltpu.PrefetchScalarGridSpec(
        num_scalar_prefetch=1,
        grid=(E, P // tp),
        in_specs=[
            pl.BlockSpec((None, tp, F), lambda e, p, rows: (e, p, 0)),
            pl.BlockSpec((None, tp, 1), lambda e, p, rows: (e, p, 0)),
            pl.BlockSpec((None, None, F, D), lambda e, p, rows: (layer, e, 0, 0), pipeline_mode=pl.Buffered(1)),
            pl.BlockSpec(memory_space=pl.ANY),
        ],
        out_specs=pl.BlockSpec(memory_space=pl.ANY),
        scratch_shapes=[pltpu.VMEM((F, D), BF16), pltpu.VMEM((2, tp, D), F32), pltpu.SemaphoreType.DMA(()),
                        pltpu.SemaphoreType.DMA((2,))],
    )
    nblk = 2 if tp % 32 == 0 else 1
    return pl.pallas_call(
        functools.partial(_moe_down_kernel, tp=tp, P=P, nblk=nblk),
        grid_spec=grid_spec,
        out_shape=jax.ShapeDtypeStruct((n_rows, D), F32),
        input_output_aliases={4: 0},
        compiler_params=_cparams(("arbitrary", "arbitrary")),
        name="moe_down_scatter",
    )(rows.reshape(-1), hid, ge, w_down, jnp.zeros((n_rows, D), F32))


def expert_choice_moe(cfg, h, logits, w_gate, w_up, w_down, layer, with_ctx):
    B, T, Tc, E = cfg.B, cfg.T, cfg.Tc, cfg.E
    n_rows = cfg.R if with_ctx else cfg.n_lat
    aff = jax.nn.softmax(logits[:n_rows, :E], axis=-1)
    sets = [(b * T, T) for b in range(B)]
    if with_ctx:
        sets += [(cfg.n_lat + b * Tc, Tc) for b in range(B)]
    rows, gates = [], []
    for start, n in sets:
        cap = max(1, (cfg.cap_factor * n) // E)
        g, idx = lax.top_k(aff[start:start + n].T, cap)
        rows.append(idx + start)
        gates.append(g)
    rows = jnp.concatenate(rows, axis=1)
    gates = jnp.concatenate(gates, axis=1)
    xe = h[rows]
    hid = moe_up(cfg, xe, w_gate, w_up, layer)
    return moe_down_scatter(cfg, rows, hid, gates[..., None], w_down, layer, n_rows)


def _forward(cfg, x, c, ctx, c_ctx, ada_w, ada_b, norm_g, attn_w_in, attn_w_out, attn_lambda, attn_subln_g,
             mlstm_w_in, mlstm_b_gates, mlstm_norm_g, mlstm_w_out, router_w, expert_w_gate, expert_w_up,
             expert_w_down):
    D, B, T, Tc, E = cfg.D, cfg.B, cfg.T, cfg.Tc, cfg.E
    depth = ada_w.shape[0]
    cos_t, sin_t = rope_tables(cfg)
    cvec = jnp.concatenate([c, c_ctx[None, :], jnp.zeros((8 - B - 1, D), F32)], axis=0)
    ada_b3 = ada_b.reshape(depth, 1, 6 * D)
    norm_g3 = norm_g.reshape(depth * 4, 1, D)
    X = (x.reshape(B * T, D), ctx.reshape(B * Tc, D))

    mods = [adaln(cvec, ada_w, ada_b3, i)[:B + 1].reshape((B + 1) * 6, 1, D) for i in range(depth)]
    rw = jnp.pad(router_w, ((0, 0), (0, 0), (0, LANES - E)))
    rw_hi = rw.astype(BF16)
    rw_lo = (rw - rw_hi.astype(F32)).astype(BF16)

    _, h, _ = resid_norm(cfg, X, norm_g3, h_part=(mods[0], 0, 0, 1, 0))
    for i in range(depth):
        last = i == depth - 1
        mod = mods[i]
        j = i // 2
        n_rows = cfg.n_lat if last else cfg.R
        if i % 2 == 0:
            lam_init = 0.8 - 0.6 * math.exp(-0.3 * i)
            w_in = attn_w_in[j]
            w_qk = rope_col_order(cfg, w_in[:, :2 * D]).astype(BF16)
            qk_lat = qk_proj_rope(cfg, h, w_qk, cos_t, sin_t)
            qk_ctx = matmul(h, w_qk, BF16, row0=cfg.n_lat, scale_cols=D,
                            scale=cfg.da_dim ** -0.5 * math.log2(math.e))
            v = matmul_wcast(h, attn_w_in, j, BF16, col0=2 * D, n_cols=D)
            g2 = attn_subln_g[j].reshape(1, -1)
            o = diff_attention(cfg, qk_lat, qk_ctx, v, attn_lambda[j], g2, lam_init, latent=True)
            if not last:
                o = (o, diff_attention(cfg, qk_lat, qk_ctx, v, attn_lambda[j], g2, lam_init, latent=False))
            y = matmul_wcast(o, attn_w_out, j, BF16)
        else:
            H = cfg.ml_heads
            nmain = mlstm_w_in.shape[2] - 4 * H
            proj = matmul_wcast(h, mlstm_w_in, j, BF16, n_cols=nmain)
            wg = jnp.pad(mlstm_w_in[j, :, nmain:], ((0, 0), (0, LANES - 4 * H))).astype(BF16)
            gates = matmul(h, wg, F32)
            bias = jnp.pad(mlstm_b_gates[j].reshape(1, 4 * H), ((0, 0), (0, LANES - 4 * H)))
            hf = mlstm_scan(cfg, proj, gates, bias, reverse=False)
            hn = mlstm_scan(cfg, proj, gates, bias, reverse=True, hf=hf, norm_g=mlstm_norm_g[j].reshape(1, D))
            y = matmul_wcast(hn, mlstm_w_out, j, BF16, n_rows=n_rows)
        X, h, lg = resid_norm(cfg, X, norm_g3, y_part=(y, mod, i, 2, 1), h_part=(mod, i, 2, 4, 3),
                              router=(rw_hi[i], rw_lo[i]), n_rows=n_rows)
        mo = expert_choice_moe(cfg, h, lg, expert_w_gate, expert_w_up, expert_w_down, i, with_ctx=not last)
        if last:
            X, _, _ = resid_norm(cfg, X, norm_g3, y_part=(mo, mod, i, 5, 3), n_rows=n_rows)
        else:
            X, h, _ = resid_norm(cfg, X, norm_g3, y_part=(mo, mod, i, 5, 3), h_part=(mods[i + 1], i + 1, 0, 1, 0))
    return X.reshape(B, T, D)


def kernel(x, c, ctx, c_ctx, ada_w, ada_b, norm_g, attn_w_in, attn_w_out, attn_lambda, attn_subln_g, mlstm_w_in,
           mlstm_b_gates, mlstm_norm_g, mlstm_w_out, router_w, expert_w_gate, expert_w_up, expert_w_down):
    return _forward(FULL_CFG, x, c, ctx, c_ctx, ada_w, ada_b, norm_g, attn_w_in, attn_w_out, attn_lambda,
                    attn_subln_g, mlstm_w_in, mlstm_b_gates, mlstm_norm_g, mlstm_w_out, router_w, expert_w_gate,
                    expert_w_up, expert_w_down)
```

```python
import functools
import math
from typing import NamedTuple

import jax
import jax.numpy as jnp
from jax import lax
from jax.experimental import pallas as pl
from jax.experimental.pallas import tpu as pltpu

BF16 = jnp.bfloat16
F32 = jnp.float32

EPS = 1e-6
GATE_CAP = 15.0
ROPE_BASE = 10000.0
LANES = 128
MIB = 1024 * 1024
VMEM_LIMIT = 56 * MIB

class Cfg(NamedTuple):
    D: int
    B: int
    T: int
    Tc: int
    grid_w: int
    da_heads: int
    da_dim: int
    ml_heads: int
    E: int
    F: int
    chunk: int
    cap_factor: int

    @property
    def n_lat(self):
        return self.B * self.T

    @property
    def n_ctx(self):
        return self.B * self.Tc

    @property
    def R(self):
        return self.n_lat + self.n_ctx

    @property
    def ml_qk(self):
        return self.D // 2 // self.ml_heads

    @property
    def ml_v(self):
        return self.D // self.ml_heads


FULL_CFG = Cfg(D=4096, B=2, T=8192, Tc=256, grid_w=64, da_heads=16, da_dim=128, ml_heads=8, E=16, F=1024,
               chunk=256, cap_factor=2)


def _cparams(sem, vmem=VMEM_LIMIT):
    return pltpu.CompilerParams(dimension_semantics=sem, vmem_limit_bytes=vmem)


def _tile(n, pref):
    t = min(n, pref)
    while n % t:
        t //= 2
    return t


def _lane_tile(x, reps):
    return x if reps == 1 else jnp.concatenate([x] * reps, axis=1)


def _adaln_kernel(c_ref, w_ref, b_ref, o_ref):
    c = c_ref[...]
    s = (c * jax.nn.sigmoid(c)).astype(BF16)
    o_ref[...] = jnp.dot(s, w_ref[...].astype(BF16), preferred_element_type=F32) + b_ref[...]


def adaln(cvec, ada_w, ada_b3):
    D = cvec.shape[1]
    depth, _, N = ada_w.shape
    tn = _tile(N, 512)
    return pl.pallas_call(
        _adaln_kernel,
        grid=(depth, N // tn),
        in_specs=[
            pl.BlockSpec((8, D), lambda l, j: (0, 0)),
            pl.BlockSpec((None, D, tn), lambda l, j: (l, 0, j)),
            pl.BlockSpec((None, 1, tn), lambda l, j: (l, 0, j)),
        ],
        out_specs=pl.BlockSpec((None, 8, tn), lambda l, j: (l, 0, j)),
        out_shape=jax.ShapeDtypeStruct((depth, 8, N), F32),
        compiler_params=_cparams(("parallel", "parallel")),
        name="adaln",
    )(cvec, ada_w, ada_b3)


def _rms(x, g):
    return x * lax.rsqrt(jnp.mean(x * x, axis=-1, keepdims=True) + EPS) * g


def _resid_norm_kernel(*refs, split_at, has_y, want_h, want_logits):
    it = iter(refs)
    x_ref = next(it)
    xc_ref = next(it) if split_at else None
    if has_y:
        y_ref, gate_ref, gpost_ref = next(it), next(it), next(it)
    if want_h:
        gpre_ref, sc_ref, sh_ref = next(it), next(it), next(it)
    if want_logits:
        rwh_ref, rwl_ref = next(it), next(it)
    if has_y:
        xo_ref = next(it)
    if want_h:
        h_ref = next(it)
    if want_logits:
        lg_ref = next(it)

    x = x_ref[...]
    if split_at:
        x = jnp.where(pl.program_id(0) < split_at, x, xc_ref[...])
    if has_y:
        y = y_ref[...].astype(F32)
        x = x + gate_ref[0] * _rms(y, gpost_ref[0])
        xo_ref[...] = x
    if want_h:
        h = _rms(x, gpre_ref[0]) * (1.0 + sc_ref[0]) + sh_ref[0]
        hb = h.astype(BF16)
        h_ref[...] = hb
        if want_logits:
            hl = (h - hb.astype(F32)).astype(BF16)
            lg_ref[...] = (jnp.dot(hb, rwh_ref[...], preferred_element_type=F32)
                           + jnp.dot(hl, rwh_ref[...], preferred_element_type=F32)
                           + jnp.dot(hb, rwl_ref[...], preferred_element_type=F32))


def resid_norm(cfg, x, norm_g, *, y_part=None, h_part=None, router=None, n_rows=None):
    D = cfg.D
    tm = _tile(cfg.Tc, 256)
    count = n_rows if n_rows is not None else cfg.R
    assert count % tm == 0
    nlat_t = cfg.n_lat // tm
    tpb = cfg.T // tm
    has_y, want_h, want_logits = y_part is not None, h_part is not None, router is not None

    def set_of(i):
        return jnp.where(i < nlat_t, i // tpb, cfg.B)

    row_spec = lambda w: pl.BlockSpec((tm, w), lambda i: (i, 0))
    mod_spec = lambda k: pl.BlockSpec((1, 1, D), lambda i: (set_of(i) * 6 + k, 0, 0))
    g_spec = lambda layer, k: pl.BlockSpec((1, 1, D), lambda i: (layer * 4 + k, 0, 0))

    split = isinstance(x, tuple)
    if split:
        args = list(x)
        in_specs = [pl.BlockSpec((tm, D), lambda i: (jnp.minimum(i, nlat_t - 1), 0)),
                    pl.BlockSpec((tm, D), lambda i: (jnp.maximum(i - nlat_t, 0), 0))]
    else:
        args, in_specs = [x], [row_spec(D)]
    out_shape, out_specs = [], []
    if has_y:
        y, mod, layer, gate_k, gpost_k = y_part
        args += [y, mod, norm_g]
        in_specs += [row_spec(D), mod_spec(gate_k), g_spec(layer, gpost_k)]
    if want_h:
        mod, layer, gpre_k, sc_k, sh_k = h_part
        args += [norm_g, mod, mod]
        in_specs += [g_spec(layer, gpre_k), mod_spec(sc_k), mod_spec(sh_k)]
    if want_logits:
        args += [router[0], router[1]]
        in_specs += [pl.BlockSpec((D, LANES), lambda i: (0, 0))] * 2
    if has_y:
        out_shape.append(jax.ShapeDtypeStruct((count, D), F32))
        out_specs.append(row_spec(D))
    if want_h:
        out_shape.append(jax.ShapeDtypeStruct((count, D), BF16))
        out_specs.append(row_spec(D))
    if want_logits:
        out_shape.append(jax.ShapeDtypeStruct((count, LANES), F32))
        out_specs.append(row_spec(LANES))

    outs = pl.pallas_call(
        functools.partial(_resid_norm_kernel, split_at=nlat_t if split else 0, has_y=has_y, want_h=want_h,
                          want_logits=want_logits),
        grid=(count // tm,),
        in_specs=in_specs,
        out_specs=out_specs,
        out_shape=out_shape,
        compiler_params=_cparams(("parallel",)),
        name="resid_norm",
    )(*args)
    outs = list(outs)
    x_new = outs.pop(0) if has_y else None
    h = outs.pop(0) if want_h else None
    lg = outs.pop(0) if want_logits else None
    return x_new, h, lg


def _mm_kernel(x_ref, w_ref, o_ref, *, scale_tiles, scale):
    acc = jnp.dot(x_ref[...], w_ref[...], preferred_element_type=F32)
    if scale_tiles:
        acc = acc * jnp.where(pl.program_id(1) < scale_tiles, scale, 1.0).astype(F32)
    o_ref[...] = acc.astype(o_ref.dtype)


def matmul(x, w, out_dtype, n_rows=None, row0=0, scale_cols=0, scale=1.0, tm_pref=1024, tn_pref=1024):
    K = x.shape[1]
    M = n_rows if n_rows is not None else x.shape[0] - row0
    N = w.shape[1]
    tm, tn = _tile(math.gcd(M, row0), tm_pref), _tile(math.gcd(N, scale_cols), tn_pref)
    assert row0 % tm == 0 and scale_cols % tn == 0
    r0 = row0 // tm
    return pl.pallas_call(
        functools.partial(_mm_kernel, scale_tiles=scale_cols // tn, scale=scale),
        grid=(M // tm, N // tn),
        in_specs=[pl.BlockSpec((tm, K), lambda i, j: (i + r0, 0)), pl.BlockSpec((K, tn), lambda i, j: (0, j))],
        out_specs=pl.BlockSpec((tm, tn), lambda i, j: (i, j)),
        out_shape=jax.ShapeDtypeStruct((M, N), out_dtype),
        compiler_params=_cparams(("parallel", "arbitrary")),
        name="matmul",
    )(x, w)


def _mm_wcast_kernel(*refs, na_tiles):
    if na_tiles:
        xa_ref, xb_ref, w_ref, o_ref, wb_ref = refs
    else:
        xa_ref, w_ref, o_ref, wb_ref = refs

    @pl.when(pl.program_id(1) == 0)
    def _():
        wb_ref[...] = w_ref[...].astype(BF16)

    x = xa_ref[...]
    if na_tiles:
        x = jnp.where(pl.program_id(1) < na_tiles, x, xb_ref[...])
    o_ref[...] = jnp.dot(x, wb_ref[...], preferred_element_type=F32).astype(o_ref.dtype)


def matmul_wcast(x, w, lead, out_dtype, n_rows=None, col0=0, n_cols=None, tm_pref=512, tn_pref=1024):
    xs = x if isinstance(x, tuple) else (x,)
    K = xs[0].shape[1]
    N = n_cols if n_cols is not None else w.shape[2]
    if len(xs) == 2:
        Ma, Mb = xs[0].shape[0], xs[1].shape[0]
        M = Ma + Mb
        tm = _tile(math.gcd(Ma, Mb), tm_pref)
        na = Ma // tm
        x_specs = [pl.BlockSpec((tm, K), lambda j, i: (jnp.minimum(i, na - 1), 0)),
                   pl.BlockSpec((tm, K), lambda j, i: (jnp.maximum(i - na, 0), 0))]
    else:
        M = n_rows if n_rows is not None else xs[0].shape[0]
        tm, na = _tile(M, tm_pref), 0
        x_specs = [pl.BlockSpec((tm, K), lambda j, i: (i, 0))]
    tn = _tile(math.gcd(N, col0), tn_pref)
    c0 = col0 // tn
    return pl.pallas_call(
        functools.partial(_mm_wcast_kernel, na_tiles=na),
        grid=(N // tn, M // tm),
        in_specs=x_specs + [pl.BlockSpec((None, K, tn), lambda j, i: (lead, 0, j + c0),
                                         pipeline_mode=pl.Buffered(1))],
        out_specs=pl.BlockSpec((tm, tn), lambda j, i: (i, j)),
        out_shape=jax.ShapeDtypeStruct((M, N), out_dtype),
        scratch_shapes=[pltpu.VMEM((K, tn), BF16)],
        compiler_params=_cparams(("parallel", "arbitrary")),
        name="matmul_wcast",
    )(*xs, w)


def _qk_rope_kernel(x_ref, w_ref, cos_ref, sin_ref, o_ref, *, q_tiles, qscale):
    scale = jnp.where(pl.program_id(1) < q_tiles, qscale, 1.0).astype(F32)
    cos, sin = cos_ref[...] * scale, sin_ref[...] * scale
    x = x_ref[...]
    tn = o_ref.shape[1]
    half = tn // 2 if (tn // 2) % LANES == 0 else tn
    for c0 in range(0, tn, half):
        acc = jnp.dot(x, w_ref[:, c0:c0 + half], preferred_element_type=F32)
        for g in range(half // LANES):
            xg = acc[:, g * LANES:(g + 1) * LANES]
            o_ref[:, c0 + g * LANES:c0 + (g + 1) * LANES] = (
                xg * cos + pltpu.roll(xg, LANES // 2, 1) * sin).astype(o_ref.dtype)


def qk_proj_rope(cfg, h, w_qk, cos_t, sin_t):
    D = cfg.D
    N = w_qk.shape[1]
    tm = _tile(cfg.T, 512)
    tn = _tile(D, 512)
    tpb = cfg.T // tm
    qscale = cfg.da_dim ** -0.5 * math.log2(math.e)
    tbl = pl.BlockSpec((tm, LANES), lambda i, j: (i % tpb, 0))
    return pl.pallas_call(
        functools.partial(_qk_rope_kernel, q_tiles=D // tn, qscale=qscale),
        grid=(cfg.n_lat // tm, N // tn),
        in_specs=[pl.BlockSpec((tm, D), lambda i, j: (i, 0)), pl.BlockSpec((D, tn), lambda i, j: (0, j)), tbl, tbl],
        out_specs=pl.BlockSpec((tm, tn), lambda i, j: (i, j)),
        out_shape=jax.ShapeDtypeStruct((cfg.n_lat, N), BF16),
        compiler_params=_cparams(("parallel", "arbitrary")),
        name="qk_proj_rope",
    )(h, w_qk, cos_t, sin_t)


def rope_col_order(cfg, w):
    K, N = w.shape
    q = cfg.da_dim // 4
    return w.reshape(K, N // cfg.da_dim, 2, 2, q).transpose(0, 1, 3, 2, 4).reshape(K, N)


def rope_tables(cfg):
    half = cfg.da_dim // 2
    t = jnp.arange(cfg.T)
    row = (t // cfg.grid_w).astype(F32)
    col = (t % cfg.grid_w).astype(F32)
    inv = ROPE_BASE ** (-jnp.arange(0, half, 2, dtype=F32) / half)
    ar, ac = row[:, None] * inv, col[:, None] * inv
    cos_t = jnp.concatenate([jnp.cos(ar), jnp.cos(ac), jnp.cos(ar), jnp.cos(ac)], axis=1)
    sin_t = jnp.concatenate([-jnp.sin(ar), -jnp.sin(ac), jnp.sin(ar), jnp.sin(ac)], axis=1)
    return cos_t, sin_t


def _attn_kernel(*refs, tk, n_lat, lam_init, dd):
    if n_lat:
        (lam_ref, g_ref, q_ref, kc_ref, vc_ref, kl_ref, vl_ref, o_ref, acc_ref, m_ref, l_ref, sa_ref,
         sb_ref) = refs
    else:
        lam_ref, g_ref, q_ref, kc_ref, vc_ref, o_ref, acc_ref, m_ref, l_ref = refs
    d = dd // 2
    q = q_ref[...]

    m_ref[...] = jnp.full(m_ref.shape, -jnp.inf, F32)
    l_ref[...] = jnp.zeros(l_ref.shape, F32)
    acc_ref[...] = jnp.zeros(acc_ref.shape, F32)

    tq = q.shape[0]
    every = slice(0, tq)

    def score(k, p, rows):
        return lax.dot_general(q[rows, p * d:(p + 1) * d], k[:, p * d:(p + 1) * d], (((1,), (1,)), ((), ())),
                               preferred_element_type=F32)

    def softmax_pv(s_of, v, rows):
        n = v.shape[0]
        nh = n // 2 if (n // 2) % LANES == 0 else n
        for p in range(2):
            m_prev = m_ref[p, rows]
            m_next = jnp.maximum(m_prev, jnp.max(s_of(p, slice(0, n)), axis=1)[:, None])
            alpha = jnp.exp2(m_prev - m_next)
            m_rep = _lane_tile(m_next, nh // LANES)
            l_new = alpha * l_ref[p, rows]
            acc = acc_ref[p, rows] * _lane_tile(alpha, dd // LANES)
            for c0 in range(0, n, nh):
                pr = jnp.exp2(s_of(p, slice(c0, c0 + nh)) - m_rep)
                l_new = l_new + jnp.sum(pr, axis=1)[:, None]
                acc = acc + jnp.dot(pr.astype(BF16), v[c0:c0 + nh, :], preferred_element_type=F32)
            l_ref[p, rows] = l_new
            m_ref[p, rows] = m_next
            acc_ref[p, rows] = acc

    kc = kc_ref[...]
    sc = [score(kc, p, every) for p in range(2)]
    if not n_lat:
        softmax_pv(lambda p, cols: sc[p][:, cols], vc_ref[...], every)
    else:
        half_a, half_b = slice(0, tq // 2), slice(tq // 2, tq)

        def chunk(ref, j):
            return ref[pl.ds(pl.multiple_of(j * tk, tk), tk), :]

        def scores_to(s_ref, j, rows):
            k = chunk(kl_ref, j)
            for p in range(2):
                s_ref[p] = score(k, p, rows)

        from_a = lambda p, cols: sa_ref[p, :, cols]
        from_b = lambda p, cols: sb_ref[p, :, cols]
        scores_to(sa_ref, 0, half_a)
        softmax_pv(lambda p, cols: sc[p][:, cols], vc_ref[...], every)

        def body(j, carry):
            scores_to(sb_ref, j, half_b)
            softmax_pv(from_a, chunk(vl_ref, j), half_a)
            scores_to(sa_ref, j + 1, half_a)
            softmax_pv(from_b, chunk(vl_ref, j), half_b)
            return carry

        lax.fori_loop(0, n_lat - 1, body, 0)
        scores_to(sb_ref, n_lat - 1, half_b)
        softmax_pv(from_a, chunk(vl_ref, n_lat - 1), half_a)
        softmax_pv(from_b, chunk(vl_ref, n_lat - 1), half_b)

    lv = lam_ref[...]
    lam = (jnp.exp(jnp.sum(lv[0:1] * lv[1:2], axis=1, keepdims=True))
           - jnp.exp(jnp.sum(lv[2:3] * lv[3:4], axis=1, keepdims=True)) + lam_init)
    o1 = acc_ref[0] / _lane_tile(l_ref[0], dd // LANES)
    o2 = acc_ref[1] / _lane_tile(l_ref[1], dd // LANES)
    o = o1 - lam * o2
    o = _rms(o, g_ref[...]) * (1.0 - lam_init)
    o_ref[...] = o.astype(o_ref.dtype)


def diff_attention(cfg, qk_lat, qk_ctx, v, lam_vecs, subln_g, lam_init, *, latent):
    D, B, T, Tc, H = cfg.D, cfg.B, cfg.T, cfg.Tc, cfg.da_heads
    dd = 2 * cfg.da_dim
    assert T % Tc == 0 and dd % LANES == 0 and Tc % LANES == 0
    ctx_blk0 = cfg.n_lat // Tc
    small = lambda shape: pl.BlockSpec(shape, lambda b, h, i: (0, 0))
    kv_ctx = [pl.BlockSpec((Tc, dd), lambda b, h, i: (b, H + h)),
              pl.BlockSpec((Tc, dd), lambda b, h, i: (ctx_blk0 + b, h))]
    if latent:
        tq, tk = _tile(T, 1024), _tile(T // 2, 1024)
        nq, n_lat = T // tq, T // tk
        assert (tq // 2) % 16 == 0
        in_specs = [small((4, cfg.da_dim)), small((1, dd)),
                    pl.BlockSpec((tq, dd), lambda b, h, i: (b * nq + i, h))] + kv_ctx + [
            pl.BlockSpec((T, dd), lambda b, h, i: (b, H + h)),
            pl.BlockSpec((T, dd), lambda b, h, i: (b, h))]
        args = [lam_vecs, subln_g, qk_lat, qk_ctx, v, qk_lat, v]
        extra = [pltpu.VMEM((2, tq // 2, tk), F32), pltpu.VMEM((2, tq // 2, tk), F32)]
    else:
        tq, tk, nq, n_lat = Tc, 0, 1, 0
        in_specs = [small((4, cfg.da_dim)), small((1, dd)),
                    pl.BlockSpec((tq, dd), lambda b, h, i: (b, h))] + kv_ctx
        args = [lam_vecs, subln_g, qk_ctx, qk_ctx, v]
        extra = []
    scratch = [pltpu.VMEM((2, tq, dd), F32), pltpu.VMEM((2, tq, LANES), F32), pltpu.VMEM((2, tq, LANES), F32)]
    return pl.pallas_call(
        functools.partial(_attn_kernel, tk=tk, n_lat=n_lat, lam_init=lam_init, dd=dd),
        grid=(B, H, nq),
        in_specs=in_specs,
        out_specs=pl.BlockSpec((tq, dd), lambda b, h, i: (b * nq + i, h)),
        out_shape=jax.ShapeDtypeStruct((B * (T if latent else Tc), D), BF16),
        scratch_shapes=scratch + extra,
        compiler_params=_cparams(("parallel", "parallel", "arbitrary")),
        name="diff_attn_lat" if latent else "diff_attn_ctx",
    )(*args)


def _time_cumsum(x, reverse):
    L = x.shape[0]
    row = lax.broadcasted_iota(jnp.int32, x.shape, 0)
    sh = 1
    while sh < L:
        if reverse:
            x = x + jnp.where(row < L - sh, pltpu.roll(x, L - sh, 0), 0.0)
        else:
            x = x + jnp.where(row >= sh, pltpu.roll(x, sh, 0), 0.0)
        sh *= 2
    return x


def _mlstm_kernel(*refs, H, dqk, dv, L, reverse, finish, gate_off):
    if finish:
        (q_ref, k_ref, v_ref, gt_ref, bias_ref, hf_ref, og_ref, ng_ref, o_ref, c_ref, n_ref, m_ref) = refs
    else:
        (q_ref, k_ref, v_ref, gt_ref, bias_ref, o_ref, c_ref, n_ref, m_ref) = refs

    @pl.when(pl.program_id(1) == 0)
    def _():
        c_ref[...] = jnp.zeros(c_ref.shape, F32)
        n_ref[...] = jnp.zeros(n_ref.shape, F32)
        m_ref[...] = jnp.zeros(m_ref.shape, F32)

    gates = GATE_CAP * jnp.tanh((gt_ref[...] + bias_ref[...]) * (1.0 / GATE_CAP))
    ig = pltpu.roll(gates, LANES - gate_off, 1) if gate_off else gates
    fg = pltpu.roll(gates, LANES - gate_off - H, 1)
    lf = jnp.minimum(fg, 0.0) - jnp.log1p(jnp.exp(-jnp.abs(fg)))
    bc = _time_cumsum(lf, reverse)
    m_prev = m_ref[0:1, :]
    a_all = bc + m_prev
    b_last = bc[0:1, :] if reverse else bc[L - 1:L, :]
    g_all = b_last - bc + ig
    m_new = jnp.maximum(b_last + m_prev, jnp.max(g_all, axis=0, keepdims=True))
    decay = jnp.exp(b_last + m_prev - m_new)
    eg = jnp.exp(g_all - m_new)
    xt = (ig - bc).T
    m_ref[...] = jnp.broadcast_to(m_new, m_ref.shape)

    li = lax.broadcasted_iota(jnp.int32, (L, L), 0)
    si = lax.broadcasted_iota(jnp.int32, (L, L), 1)
    tri = (si >= li) if reverse else (si <= li)
    qscale = dqk ** -0.5

    for h in range(H):
        q = (q_ref[:, h * dqk:(h + 1) * dqk].astype(F32) * qscale).astype(BF16)
        k = k_ref[:, h * dqk:(h + 1) * dqk]
        v = v_ref[:, h * dv:(h + 1) * dv]
        ct = c_ref[h]
        nv = n_ref[h]
        a_h = a_all[:, h:h + 1]
        dmat = jnp.where(tri, bc[:, h:h + 1] + xt[h:h + 1, :], -jnp.inf)
        m_t = jnp.maximum(a_h, jnp.max(dmat, axis=1, keepdims=True))
        qk = lax.dot_general(q, k, (((1,), (1,)), ((), ())), preferred_element_type=F32)
        w = jnp.exp(dmat - m_t) * qk
        inter = jnp.exp(a_h - m_t)
        num = inter * jnp.dot(q, ct.astype(BF16), preferred_element_type=F32) + jnp.dot(
            w.astype(BF16), v, preferred_element_type=F32)
        qn = jnp.sum(q.astype(F32) * nv, axis=1, keepdims=True)
        den = inter * qn + jnp.sum(w, axis=1, keepdims=True)
        hout = num / jnp.maximum(jnp.abs(den), jnp.exp(-m_t))

        dec_h = decay[:, h:h + 1]
        wk = eg[:, h:h + 1] * k.astype(F32)
        c_ref[h] = dec_h * ct + lax.dot_general(wk.astype(BF16), v, (((0,), (0,)), ((), ())),
                                                preferred_element_type=F32)
        n_ref[h] = dec_h * nv + jnp.sum(wk, axis=0, keepdims=True)

        if finish:
            hs = hout + hf_ref[:, h * dv:(h + 1) * dv]
            hn = _rms(hs, ng_ref[:, h * dv:(h + 1) * dv]).astype(BF16)
            og = og_ref[:, h * dv:(h + 1) * dv]
            o_ref[:, h * dv:(h + 1) * dv] = (hn * jax.nn.sigmoid(og)).astype(o_ref.dtype)
        else:
            o_ref[:, h * dv:(h + 1) * dv] = hout.astype(o_ref.dtype)


def mlstm_scan(cfg, proj, gates, bias, *, reverse, hf=None, norm_g=None):
    D, B, T, Tc, H, L = cfg.D, cfg.B, cfg.T, cfg.Tc, cfg.ml_heads, cfg.chunk
    dqk, dv = cfg.ml_qk, cfg.ml_v
    ncc, ncl = Tc // L, T // L
    nc = ncc + ncl
    finish = hf is not None
    qw = H * dqk
    assert D % qw == 0 and (H * dv) == D

    def rb(b, c):
        if reverse:
            return jnp.where(c < ncc, cfg.n_lat // L + b * ncc + (ncc - 1 - c), b * ncl + (nc - 1 - c))
        return jnp.where(c < ncc, cfg.n_lat // L + b * ncc + c, b * ncl + (c - ncc))

    in_specs = [
        pl.BlockSpec((L, qw), lambda b, c: (rb(b, c), 0)),
        pl.BlockSpec((L, qw), lambda b, c: (rb(b, c), 1)),
        pl.BlockSpec((L, D), lambda b, c: (rb(b, c), 2 * qw // D)),
        pl.BlockSpec((L, LANES), lambda b, c: (rb(b, c), 0)),
        pl.BlockSpec((1, LANES), lambda b, c: (0, 0)),
    ]
    args = [proj, proj, proj, gates, bias]
    if finish:
        in_specs += [
            pl.BlockSpec((L, D), lambda b, c: (rb(b, c), 0)),
            pl.BlockSpec((L, D), lambda b, c: (rb(b, c), 2 * qw // D + 1)),
            pl.BlockSpec((1, D), lambda b, c: (0, 0)),
        ]
        args += [hf, proj, norm_g]
    return pl.pallas_call(
        functools.partial(_mlstm_kernel, H=H, dqk=dqk, dv=dv, L=L, reverse=reverse, finish=finish,
                          gate_off=2 * H if reverse else 0),
        grid=(B, nc),
        in_specs=in_specs,
        out_specs=pl.BlockSpec((L, D), lambda b, c: (rb(b, c), 0)),
        out_shape=jax.ShapeDtypeStruct((cfg.R, D), BF16 if finish else F32),
        scratch_shapes=[pltpu.VMEM((H, dqk, dv), F32), pltpu.VMEM((H, 1, dqk), F32), pltpu.VMEM((8, LANES), F32)],
        compiler_params=_cparams(("parallel", "arbitrary")),
        name="mlstm_bwd" if reverse else "mlstm_fwd",
    )(*args)


def _moe_up_kernel(x_ref, wg_ref, wu_ref, o_ref, wgb_ref, wub_ref):
    @pl.when(pl.program_id(2) == 0)
    def _():
        wgb_ref[...] = wg_ref[...].astype(BF16)
        wub_ref[...] = wu_ref[...].astype(BF16)

    x = x_ref[...]
    a = jnp.dot(x, wgb_ref[...], preferred_element_type=F32)
    u = jnp.dot(x, wub_ref[...], preferred_element_type=F32)
    o_ref[...] = (a * jax.nn.sigmoid(a) * u).astype(o_ref.dtype)


def _row_tile(n, cap):
    best = n
    for t in range(16, min(n, cap) + 1, 16):
        if n % t == 0:
            best = t
    return best


def moe_up(cfg, xe, w_gate, w_up, layer):
    E, P, D = xe.shape
    F = cfg.F
    tp = _row_tile(P, 528)
    fk = _tile(F, 512)
    w_spec = pl.BlockSpec((None, None, D, fk), lambda e, f, p: (layer, e, 0, f))
    return pl.pallas_call(
        _moe_up_kernel,
        grid=(E, F // fk, P // tp),
        in_specs=[pl.BlockSpec((None, tp, D), lambda e, f, p: (e, p, 0)), w_spec, w_spec],
        out_specs=pl.BlockSpec((None, tp, fk), lambda e, f, p: (e, p, f)),
        out_shape=jax.ShapeDtypeStruct((E, P, F), BF16),
        scratch_shapes=[pltpu.VMEM((D, fk), BF16), pltpu.VMEM((D, fk), BF16)],
        compiler_params=_cparams(("parallel", "parallel", "arbitrary")),
        name="moe_up",
    )(xe, w_gate, w_up)


def _moe_down_kernel(rows_ref, hid_ref, g_ref, wd_ref, acc_in_ref, acc_ref, wdb_ref, buf_ref, sem_in, sem_out, *,
                     tp, P, nblk):
    del acc_in_ref
    e, p = pl.program_id(0), pl.program_id(1)
    step = e * pl.num_programs(1) + p
    last = pl.num_programs(0) * pl.num_programs(1) - 1
    slot = lax.rem(step, 2)
    base = e * P + p * tp
    blk = tp // nblk

    def tile_copy(s, sem):
        return pltpu.make_async_copy(acc_ref.at[pl.ds(0, tp)], buf_ref.at[s], sem)

    @pl.when(p == 0)
    def _():
        wdb_ref[...] = wd_ref[...].astype(BF16)

        @pl.when(step > 0)
        def _():
            tile_copy(1 - slot, sem_out.at[1 - slot]).wait()

    ys = []
    for i in range(nblk):
        r0 = i * blk
        for r in range(r0, r0 + blk):
            pltpu.make_async_copy(acc_ref.at[pl.ds(rows_ref[base + r], 1)], buf_ref.at[slot, pl.ds(r, 1)],
                                  sem_in).start(priority=r % 2)
        ys.append(jnp.dot(hid_ref[r0:r0 + blk, :], wdb_ref[...], preferred_element_type=F32)
                  * g_ref[r0:r0 + blk, :])
    tile_copy(slot, sem_in).wait()
    for i in range(nblk):
        r0 = i * blk
        buf_ref[slot, r0:r0 + blk, :] += ys[i]
        for r in range(r0, r0 + blk):
            pltpu.make_async_copy(buf_ref.at[slot, pl.ds(r, 1)], acc_ref.at[pl.ds(rows_ref[base + r], 1)],
                                  sem_out.at[slot]).start(priority=r % 2)

    @pl.when(jnp.logical_and(p != 0, step > 0))
    def _():
        tile_copy(1 - slot, sem_out.at[1 - slot]).wait()

    @pl.when(step == last)
    def _():
        tile_copy(slot, sem_out.at[slot]).wait()


def moe_down_scatter(cfg, rows, hid, ge, w_down, layer, n_rows):
    E, P, F = hid.shape
    D = cfg.D
    tp = _row_tile(P, 352)
    grid_spec = pltpu.PrefetchScalarGridSpec(
        num_scalar_prefetch=1,
        grid=(E, P // tp),
        in_specs=[
            pl.BlockSpec((None, tp, F), lambda e, p, rows: (e, p, 0)),
            pl.BlockSpec((None, tp, 1), lambda e, p, rows: (e, p, 0)),
            pl.BlockSpec((None, None, F, D), lambda e, p, rows: (layer, e, 0, 0), pipeline_mode=pl.Buffered(1)),
            pl.BlockSpec(memory_space=pl.ANY),
        ],
        out_specs=pl.BlockSpec(memory_space=pl.ANY),
        scratch_shapes=[pltpu.VMEM((F, D), BF16), pltpu.VMEM((2, tp, D), F32), pltpu.SemaphoreType.DMA(()),
                        pltpu.SemaphoreType.DMA((2,))],
    )
    nblk = 2 if tp % 32 == 0 else 1
    return pl.pallas_call(
        functools.partial(_moe_down_kernel, tp=tp, P=P, nblk=nblk),
        grid_spec=grid_spec,
        out_shape=jax.ShapeDtypeStruct((n_rows, D), F32),
        input_output_aliases={4: 0},
        compiler_params=_cparams(("arbitrary", "arbitrary")),
        name="moe_down_scatter",
    )(rows.reshape(-1), hid, ge, w_down, jnp.zeros((n_rows, D), F32))


def expert_choice_moe(cfg, h, logits, w_gate, w_up, w_down, layer, with_ctx):
    B, T, Tc, E = cfg.B, cfg.T, cfg.Tc, cfg.E
    n_rows = cfg.R if with_ctx else cfg.n_lat
    aff = jax.nn.softmax(logits[:n_rows, :E], axis=-1)
    sets = [(b * T, T) for b in range(B)]
    if with_ctx:
        sets += [(cfg.n_lat + b * Tc, Tc) for b in range(B)]
    rows, gates = [], []
    for start, n in sets:
        cap = max(1, (cfg.cap_factor * n) // E)
        g, idx = lax.top_k(aff[start:start + n].T, cap)
        rows.append(idx + start)
        gates.append(g)
    rows = jnp.concatenate(rows, axis=1)
    gates = jnp.concatenate(gates, axis=1)
    xe = h[rows]
    hid = moe_up(cfg, xe, w_gate, w_up, layer)
    return moe_down_scatter(cfg, rows, hid, gates[..., None], w_down, layer, n_rows)


def _forward(cfg, x, c, ctx, c_ctx, ada_w, ada_b, norm_g, attn_w_in, attn_w_out, attn_lambda, attn_subln_g,
             mlstm_w_in, mlstm_b_gates, mlstm_norm_g, mlstm_w_out, router_w, expert_w_gate, expert_w_up,
             expert_w_down):
    D, B, T, Tc, E = cfg.D, cfg.B, cfg.T, cfg.Tc, cfg.E
    depth = ada_w.shape[0]
    cos_t, sin_t = rope_tables(cfg)
    cvec = jnp.concatenate([c, c_ctx[None, :], jnp.zeros((8 - B - 1, D), F32)], axis=0)
    ada_b3 = ada_b.reshape(depth, 1, 6 * D)
    norm_g3 = norm_g.reshape(depth * 4, 1, D)
    X = (x.reshape(B * T, D), ctx.reshape(B * Tc, D))

    mod_all = adaln(cvec, ada_w, ada_b3)
    mods = [mod_all[i, :B + 1].reshape((B + 1) * 6, 1, D) for i in range(depth)]
    rw = jnp.pad(router_w, ((0, 0), (0, 0), (0, LANES - E)))
    rw_hi = rw.astype(BF16)
    rw_lo = (rw - rw_hi.astype(F32)).astype(BF16)

    _, h, _ = resid_norm(cfg, X, norm_g3, h_part=(mods[0], 0, 0, 1, 0))
    for i in range(depth):
        last = i == depth - 1
        mod = mods[i]
        j = i // 2
        n_rows = cfg.n_lat if last else cfg.R
        if i % 2 == 0:
            lam_init = 0.8 - 0.6 * math.exp(-0.3 * i)
            w_in = attn_w_in[j]
            w_qk = rope_col_order(cfg, w_in[:, :2 * D]).astype(BF16)
            qk_lat = qk_proj_rope(cfg, h, w_qk, cos_t, sin_t)
            qk_ctx = matmul(h, w_qk, BF16, row0=cfg.n_lat, scale_cols=D,
                            scale=cfg.da_dim ** -0.5 * math.log2(math.e))
            v = matmul_wcast(h, attn_w_in, j, BF16, col0=2 * D, n_cols=D)
            g2 = attn_subln_g[j].reshape(1, -1)
            o = diff_attention(cfg, qk_lat, qk_ctx, v, attn_lambda[j], g2, lam_init, latent=True)
            if not last:
                o = (o, diff_attention(cfg, qk_lat, qk_ctx, v, attn_lambda[j], g2, lam_init, latent=False))
            y = matmul_wcast(o, attn_w_out, j, BF16)
        else:
            H = cfg.ml_heads
            nmain = mlstm_w_in.shape[2] - 4 * H
            proj = matmul_wcast(h, mlstm_w_in, j, BF16, n_cols=nmain)
            wg = jnp.pad(mlstm_w_in[j, :, nmain:], ((0, 0), (0, LANES - 4 * H))).astype(BF16)
            gates = matmul(h, wg, F32)
            bias = jnp.pad(mlstm_b_gates[j].reshape(1, 4 * H), ((0, 0), (0, LANES - 4 * H)))
            hf = mlstm_scan(cfg, proj, gates, bias, reverse=False)
            hn = mlstm_scan(cfg, proj, gates, bias, reverse=True, hf=hf, norm_g=mlstm_norm_g[j].reshape(1, D))
            y = matmul_wcast(hn, mlstm_w_out, j, BF16, n_rows=n_rows)
        X, h, lg = resid_norm(cfg, X, norm_g3, y_part=(y, mod, i, 2, 1), h_part=(mod, i, 2, 4, 3),
                              router=(rw_hi[i], rw_lo[i]), n_rows=n_rows)
        mo = expert_choice_moe(cfg, h, lg, expert_w_gate, expert_w_up, expert_w_down, i, with_ctx=not last)
        if last:
            X, _, _ = resid_norm(cfg, X, norm_g3, y_part=(mo, mod, i, 5, 3), n_rows=n_rows)
        else:
            X, h, _ = resid_norm(cfg, X, norm_g3, y_part=(mo, mod, i, 5, 3), h_part=(mods[i + 1], i + 1, 0, 1, 0))
    return X.reshape(B, T, D)


def kernel(x, c, ctx, c_ctx, ada_w, ada_b, norm_g, attn_w_in, attn_w_out, attn_lambda, attn_subln_g, mlstm_w_in,
           mlstm_b_gates, mlstm_norm_g, mlstm_w_out, router_w, expert_w_gate, expert_w_up, expert_w_down):
    return _forward(FULL_CFG, x, c, ctx, c_ctx, ada_w, ada_b, norm_g, attn_w_in, attn_w_out, attn_lambda,
                    attn_subln_g, mlstm_w_in, mlstm_b_gates, mlstm_norm_g, mlstm_w_out, router_w, expert_w_gate,
                    expert_w_up, expert_w_down)
```

```python
import functools
import math
from typing import NamedTuple

import jax
import jax.numpy as jnp
from jax import lax
from jax.experimental import pallas as pl
from jax.experimental.pallas import tpu as pltpu

BF16 = jnp.bfloat16
F32 = jnp.float32

EPS = 1e-6
GATE_CAP = 15.0
ROPE_BASE = 10000.0
LANES = 128
MIB = 1024 * 1024
VMEM_LIMIT = 56 * MIB

class Cfg(NamedTuple):
    D: int
    B: int
    T: int
    Tc: int
    grid_w: int
    da_heads: int
    da_dim: int
    ml_heads: int
    E: int
    F: int
    chunk: int
    cap_factor: int

    @property
    def n_lat(self):
        return self.B * self.T

    @property
    def n_ctx(self):
        return self.B * self.Tc

    @property
    def R(self):
        return self.n_lat + self.n_ctx

    @property
    def ml_qk(self):
        return self.D // 2 // self.ml_heads

    @property
    def ml_v(self):
        return self.D // self.ml_heads


FULL_CFG = Cfg(D=4096, B=2, T=8192, Tc=256, grid_w=64, da_heads=16, da_dim=128, ml_heads=8, E=16, F=1024,
               chunk=256, cap_factor=2)


def _cparams(sem, vmem=VMEM_LIMIT):
    return pltpu.CompilerParams(dimension_semantics=sem, vmem_limit_bytes=vmem)


def _tile(n, pref):
    t = min(n, pref)
    while n % t:
        t //= 2
    return t


def _lane_tile(x, reps):
    return x if reps == 1 else jnp.concatenate([x] * reps, axis=1)


def _adaln_kernel(c_ref, w_ref, b_ref, o_ref):
    c = c_ref[...]
    s = (c * jax.nn.sigmoid(c)).astype(BF16)
    o_ref[...] = jnp.dot(s, w_ref[...].astype(BF16), preferred_element_type=F32) + b_ref[...]


def adaln(cvec, ada_w, ada_b3, layer):
    D = cvec.shape[1]
    N = ada_w.shape[2]
    tn = _tile(N, 512)
    return pl.pallas_call(
        _adaln_kernel,
        grid=(N // tn,),
        in_specs=[
            pl.BlockSpec((8, D), lambda j: (0, 0)),
            pl.BlockSpec((None, D, tn), lambda j: (layer, 0, j)),
            pl.BlockSpec((None, 1, tn), lambda j: (layer, 0, j)),
        ],
        out_specs=pl.BlockSpec((8, tn), lambda j: (0, j)),
        out_shape=jax.ShapeDtypeStruct((8, N), F32),
        compiler_params=_cparams(("parallel",)),
        name="adaln",
    )(cvec, ada_w, ada_b3)


def _rms(x, g):
    return x * lax.rsqrt(jnp.mean(x * x, axis=-1, keepdims=True) + EPS) * g


def _resid_norm_kernel(*refs, split_at, has_y, want_h, want_logits):
    it = iter(refs)
    x_ref = next(it)
    xc_ref = next(it) if split_at else None
    if has_y:
        y_ref, gate_ref, gpost_ref = next(it), next(it), next(it)
    if want_h:
        gpre_ref, sc_ref, sh_ref = next(it), next(it), next(it)
    if want_logits:
        rwh_ref, rwl_ref = next(it), next(it)
    if has_y:
        xo_ref = next(it)
    if want_h:
        h_ref = next(it)
    if want_logits:
        lg_ref = next(it)

    x = x_ref[...]
    if split_at:
        x = jnp.where(pl.program_id(0) < split_at, x, xc_ref[...])
    if has_y:
        y = y_ref[...].astype(F32)
        x = x + gate_ref[0] * _rms(y, gpost_ref[0])
        xo_ref[...] = x
    if want_h:
        h = _rms(x, gpre_ref[0]) * (1.0 + sc_ref[0]) + sh_ref[0]
        hb = h.astype(BF16)
        h_ref[...] = hb
        if want_logits:
            hl = (h - hb.astype(F32)).astype(BF16)
            lg_ref[...] = (jnp.dot(hb, rwh_ref[...], preferred_element_type=F32)
                           + jnp.dot(hl, rwh_ref[...], preferred_element_type=F32)
                           + jnp.dot(hb, rwl_ref[...], preferred_element_type=F32))


def resid_norm(cfg, x, norm_g, *, y_part=None, h_part=None, router=None, n_rows=None):
    D = cfg.D
    tm = _tile(cfg.Tc, 256)
    count = n_rows if n_rows is not None else cfg.R
    assert count % tm == 0
    nlat_t = cfg.n_lat // tm
    tpb = cfg.T // tm
    has_y, want_h, want_logits = y_part is not None, h_part is not None, router is not None

    def set_of(i):
        return jnp.where(i < nlat_t, i // tpb, cfg.B)

    row_spec = lambda w: pl.BlockSpec((tm, w), lambda i: (i, 0))
    mod_spec = lambda k: pl.BlockSpec((1, 1, D), lambda i: (set_of(i) * 6 + k, 0, 0))
    g_spec = lambda layer, k: pl.BlockSpec((1, 1, D), lambda i: (layer * 4 + k, 0, 0))

    split = isinstance(x, tuple)
    if split:
        args = list(x)
        in_specs = [pl.BlockSpec((tm, D), lambda i: (jnp.minimum(i, nlat_t - 1), 0)),
                    pl.BlockSpec((tm, D), lambda i: (jnp.maximum(i - nlat_t, 0), 0))]
    else:
        args, in_specs = [x], [row_spec(D)]
    out_shape, out_specs = [], []
    if has_y:
        y, mod, layer, gate_k, gpost_k = y_part
        args += [y, mod, norm_g]
        in_specs += [row_spec(D), mod_spec(gate_k), g_spec(layer, gpost_k)]
    if want_h:
        mod, layer, gpre_k, sc_k, sh_k = h_part
        args += [norm_g, mod, mod]
        in_specs += [g_spec(layer, gpre_k), mod_spec(sc_k), mod_spec(sh_k)]
    if want_logits:
        args += [router[0], router[1]]
        in_specs += [pl.BlockSpec((D, LANES), lambda i: (0, 0))] * 2
    if has_y:
        out_shape.append(jax.ShapeDtypeStruct((count, D), F32))
        out_specs.append(row_spec(D))
    if want_h:
        out_shape.append(jax.ShapeDtypeStruct((count, D), BF16))
        out_specs.append(row_spec(D))
    if want_logits:
        out_shape.append(jax.ShapeDtypeStruct((count, LANES), F32))
        out_specs.append(row_spec(LANES))

    outs = pl.pallas_call(
        functools.partial(_resid_norm_kernel, split_at=nlat_t if split else 0, has_y=has_y, want_h=want_h,
                          want_logits=want_logits),
        grid=(count // tm,),
        in_specs=in_specs,
        out_specs=out_specs,
        out_shape=out_shape,
        compiler_params=_cparams(("parallel",)),
        name="resid_norm",
    )(*args)
    outs = list(outs)
    x_new = outs.pop(0) if has_y else None
    h = outs.pop(0) if want_h else None
    lg = outs.pop(0) if want_logits else None
    return x_new, h, lg


def _mm_kernel(x_ref, w_ref, o_ref, *, scale_tiles, scale):
    acc = jnp.dot(x_ref[...], w_ref[...], preferred_element_type=F32)
    if scale_tiles:
        acc = acc * jnp.where(pl.program_id(1) < scale_tiles, scale, 1.0).astype(F32)
    o_ref[...] = acc.astype(o_ref.dtype)


def matmul(x, w, out_dtype, n_rows=None, row0=0, scale_cols=0, scale=1.0, tm_pref=1024, tn_pref=1024):
    K = x.shape[1]
    M = n_rows if n_rows is not None else x.shape[0] - row0
    N = w.shape[1]
    tm, tn = _tile(math.gcd(M, row0), tm_pref), _tile(math.gcd(N, scale_cols), tn_pref)
    assert row0 % tm == 0 and scale_cols % tn == 0
    r0 = row0 // tm
    return pl.pallas_call(
        functools.partial(_mm_kernel, scale_tiles=scale_cols // tn, scale=scale),
        grid=(M // tm, N // tn),
        in_specs=[pl.BlockSpec((tm, K), lambda i, j: (i + r0, 0)), pl.BlockSpec((K, tn), lambda i, j: (0, j))],
        out_specs=pl.BlockSpec((tm, tn), lambda i, j: (i, j)),
        out_shape=jax.ShapeDtypeStruct((M, N), out_dtype),
        compiler_params=_cparams(("parallel", "arbitrary")),
        name="matmul",
    )(x, w)


def _mm_wcast_kernel(*refs, na_tiles):
    if na_tiles:
        xa_ref, xb_ref, w_ref, o_ref, wb_ref = refs
    else:
        xa_ref, w_ref, o_ref, wb_ref = refs

    @pl.when(pl.program_id(1) == 0)
    def _():
        wb_ref[...] = w_ref[...].astype(BF16)

    x = xa_ref[...]
    if na_tiles:
        x = jnp.where(pl.program_id(1) < na_tiles, x, xb_ref[...])
    o_ref[...] = jnp.dot(x, wb_ref[...], preferred_element_type=F32).astype(o_ref.dtype)


def matmul_wcast(x, w, lead, out_dtype, n_rows=None, col0=0, n_cols=None, tm_pref=512, tn_pref=1024):
    xs = x if isinstance(x, tuple) else (x,)
    K = xs[0].shape[1]
    N = n_cols if n_cols is not None else w.shape[2]
    if len(xs) == 2:
        Ma, Mb = xs[0].shape[0], xs[1].shape[0]
        M = Ma + Mb
        tm = _tile(math.gcd(Ma, Mb), tm_pref)
        na = Ma // tm
        x_specs = [pl.BlockSpec((tm, K), lambda j, i: (jnp.minimum(i, na - 1), 0)),
                   pl.BlockSpec((tm, K), lambda j, i: (jnp.maximum(i - na, 0), 0))]
    else:
        M = n_rows if n_rows is not None else xs[0].shape[0]
        tm, na = _tile(M, tm_pref), 0
        x_specs = [pl.BlockSpec((tm, K), lambda j, i: (i, 0))]
    tn = _tile(math.gcd(N, col0), tn_pref)
    c0 = col0 // tn
    return pl.pallas_call(
        functools.partial(_mm_wcast_kernel, na_tiles=na),
        grid=(N // tn, M // tm),
        in_specs=x_specs + [pl.BlockSpec((None, K, tn), lambda j, i: (lead, 0, j + c0),
                                         pipeline_mode=pl.Buffered(1))],
        out_specs=pl.BlockSpec((tm, tn), lambda j, i: (i, j)),
        out_shape=jax.ShapeDtypeStruct((M, N), out_dtype),
        scratch_shapes=[pltpu.VMEM((K, tn), BF16)],
        compiler_params=_cparams(("parallel", "arbitrary")),
        name="matmul_wcast",
    )(*xs, w)


def _qk_rope_kernel(x_ref, w_ref, cos_ref, sin_ref, o_ref, *, q_tiles, qscale):
    scale = jnp.where(pl.program_id(1) < q_tiles, qscale, 1.0).astype(F32)
    cos, sin = cos_ref[...] * scale, sin_ref[...] * scale
    x = x_ref[...]
    tn = o_ref.shape[1]
    half = tn // 2 if (tn // 2) % LANES == 0 else tn
    for c0 in range(0, tn, half):
        acc = jnp.dot(x, w_ref[:, c0:c0 + half], preferred_element_type=F32)
        for g in range(half // LANES):
            xg = acc[:, g * LANES:(g + 1) * LANES]
            o_ref[:, c0 + g * LANES:c0 + (g + 1) * LANES] = (
                xg * cos + pltpu.roll(xg, LANES // 2, 1) * sin).astype(o_ref.dtype)


def qk_proj_rope(cfg, h, w_qk, cos_t, sin_t):
    D = cfg.D
    N = w_qk.shape[1]
    tm = _tile(cfg.T, 512)
    tn = _tile(D, 512)
    tpb = cfg.T // tm
    qscale = cfg.da_dim ** -0.5 * math.log2(math.e)
    tbl = pl.BlockSpec((tm, LANES), lambda i, j: (i % tpb, 0))
    return pl.pallas_call(
        functools.partial(_qk_rope_kernel, q_tiles=D // tn, qscale=qscale),
        grid=(cfg.n_lat // tm, N // tn),
        in_specs=[pl.BlockSpec((tm, D), lambda i, j: (i, 0)), pl.BlockSpec((D, tn), lambda i, j: (0, j)), tbl, tbl],
        out_specs=pl.BlockSpec((tm, tn), lambda i, j: (i, j)),
        out_shape=jax.ShapeDtypeStruct((cfg.n_lat, N), BF16),
        compiler_params=_cparams(("parallel", "arbitrary")),
        name="qk_proj_rope",
    )(h, w_qk, cos_t, sin_t)


def rope_col_order(cfg, w):
    K, N = w.shape
    q = cfg.da_dim // 4
    return w.reshape(K, N // cfg.da_dim, 2, 2, q).transpose(0, 1, 3, 2, 4).reshape(K, N)


def rope_tables(cfg):
    half = cfg.da_dim // 2
    t = jnp.arange(cfg.T)
    row = (t // cfg.grid_w).astype(F32)
    col = (t % cfg.grid_w).astype(F32)
    inv = ROPE_BASE ** (-jnp.arange(0, half, 2, dtype=F32) / half)
    ar, ac = row[:, None] * inv, col[:, None] * inv
    cos_t = jnp.concatenate([jnp.cos(ar), jnp.cos(ac), jnp.cos(ar), jnp.cos(ac)], axis=1)
    sin_t = jnp.concatenate([-jnp.sin(ar), -jnp.sin(ac), jnp.sin(ar), jnp.sin(ac)], axis=1)
    return cos_t, sin_t


def _attn_kernel(*refs, tk, n_lat, lam_init, dd):
    if n_lat:
        (lam_ref, g_ref, q_ref, kc_ref, vc_ref, kl_ref, vl_ref, o_ref, acc_ref, m_ref, l_ref, sa_ref,
         sb_ref) = refs
    else:
        lam_ref, g_ref, q_ref, kc_ref, vc_ref, o_ref, acc_ref, m_ref, l_ref = refs
    d = dd // 2
    q = q_ref[...]

    m_ref[...] = jnp.full(m_ref.shape, -jnp.inf, F32)
    l_ref[...] = jnp.zeros(l_ref.shape, F32)
    acc_ref[...] = jnp.zeros(acc_ref.shape, F32)

    tq = q.shape[0]
    every = slice(0, tq)

    def score(k, p, rows):
        return lax.dot_general(q[rows, p * d:(p + 1) * d], k[:, p * d:(p + 1) * d], (((1,), (1,)), ((), ())),
                               preferred_element_type=F32)

    def softmax_pv(s_of, v, rows):
        n = v.shape[0]
        nh = n // 2 if (n // 2) % LANES == 0 else n
        for p in range(2):
            m_prev = m_ref[p, rows]
            m_next = jnp.maximum(m_prev, jnp.max(s_of(p, slice(0, n)), axis=1)[:, None])
            alpha = jnp.exp2(m_prev - m_next)
            m_rep = _lane_tile(m_next, nh // LANES)
            l_new = alpha * l_ref[p, rows]
            acc = acc_ref[p, rows] * _lane_tile(alpha, dd // LANES)
            for c0 in range(0, n, nh):
                pr = jnp.exp2(s_of(p, slice(c0, c0 + nh)) - m_rep)
                l_new = l_new + jnp.sum(pr, axis=1)[:, None]
                acc = acc + jnp.dot(pr.astype(BF16), v[c0:c0 + nh, :], preferred_element_type=F32)
            l_ref[p, rows] = l_new
            m_ref[p, rows] = m_next
            acc_ref[p, rows] = acc

    kc = kc_ref[...]
    sc = [score(kc, p, every) for p in range(2)]
    if not n_lat:
        softmax_pv(lambda p, cols: sc[p][:, cols], vc_ref[...], every)
    else:
        half_a, half_b = slice(0, tq // 2), slice(tq // 2, tq)

        def chunk(ref, j):
            return ref[pl.ds(pl.multiple_of(j * tk, tk), tk), :]

        def scores_to(s_ref, j, rows):
            k = chunk(kl_ref, j)
            for p in range(2):
                s_ref[p] = score(k, p, rows)

        from_a = lambda p, cols: sa_ref[p, :, cols]
        from_b = lambda p, cols: sb_ref[p, :, cols]
        scores_to(sa_ref, 0, half_a)
        softmax_pv(lambda p, cols: sc[p][:, cols], vc_ref[...], every)

        def body(j, carry):
            scores_to(sb_ref, j, half_b)
            softmax_pv(from_a, chunk(vl_ref, j), half_a)
            scores_to(sa_ref, j + 1, half_a)
            softmax_pv(from_b, chunk(vl_ref, j), half_b)
            return carry

        lax.fori_loop(0, n_lat - 1, body, 0)
        scores_to(sb_ref, n_lat - 1, half_b)
        softmax_pv(from_a, chunk(vl_ref, n_lat - 1), half_a)
        softmax_pv(from_b, chunk(vl_ref, n_lat - 1), half_b)

    lv = lam_ref[...]
    lam = (jnp.exp(jnp.sum(lv[0:1] * lv[1:2], axis=1, keepdims=True))
           - jnp.exp(jnp.sum(lv[2:3] * lv[3:4], axis=1, keepdims=True)) + lam_init)
    o1 = acc_ref[0] / _lane_tile(l_ref[0], dd // LANES)
    o2 = acc_ref[1] / _lane_tile(l_ref[1], dd // LANES)
    o = o1 - lam * o2
    o = _rms(o, g_ref[...]) * (1.0 - lam_init)
    o_ref[...] = o.astype(o_ref.dtype)


def diff_attention(cfg, qk_lat, qk_ctx, v, lam_vecs, subln_g, lam_init, *, latent):
    D, B, T, Tc, H = cfg.D, cfg.B, cfg.T, cfg.Tc, cfg.da_heads
    dd = 2 * cfg.da_dim
    assert T % Tc == 0 and dd % LANES == 0 and Tc % LANES == 0
    ctx_blk0 = cfg.n_lat // Tc
    small = lambda shape: pl.BlockSpec(shape, lambda b, h, i: (0, 0))
    kv_ctx = [pl.BlockSpec((Tc, dd), lambda b, h, i: (b, H + h)),
              pl.BlockSpec((Tc, dd), lambda b, h, i: (ctx_blk0 + b, h))]
    if latent:
        tq, tk = _tile(T, 1024), _tile(T // 2, 1024)
        nq, n_lat = T // tq, T // tk
        assert (tq // 2) % 16 == 0
        in_specs = [small((4, cfg.da_dim)), small((1, dd)),
                    pl.BlockSpec((tq, dd), lambda b, h, i: (b * nq + i, h))] + kv_ctx + [
            pl.BlockSpec((T, dd), lambda b, h, i: (b, H + h)),
            pl.BlockSpec((T, dd), lambda b, h, i: (b, h))]
        args = [lam_vecs, subln_g, qk_lat, qk_ctx, v, qk_lat, v]
        extra = [pltpu.VMEM((2, tq // 2, tk), F32), pltpu.VMEM((2, tq // 2, tk), F32)]
    else:
        tq, tk, nq, n_lat = Tc, 0, 1, 0
        in_specs = [small((4, cfg.da_dim)), small((1, dd)),
                    pl.BlockSpec((tq, dd), lambda b, h, i: (b, h))] + kv_ctx
        args = [lam_vecs, subln_g, qk_ctx, qk_ctx, v]
        extra = []
    scratch = [pltpu.VMEM((2, tq, dd), F32), pltpu.VMEM((2, tq, LANES), F32), pltpu.VMEM((2, tq, LANES), F32)]
    return pl.pallas_call(
        functools.partial(_attn_kernel, tk=tk, n_lat=n_lat, lam_init=lam_init, dd=dd),
        grid=(B, H, nq),
        in_specs=in_specs,
        out_specs=pl.BlockSpec((tq, dd), lambda b, h, i: (b * nq + i, h)),
        out_shape=jax.ShapeDtypeStruct((B * (T if latent else Tc), D), BF16),
        scratch_shapes=scratch + extra,
        compiler_params=_cparams(("parallel", "parallel", "arbitrary")),
        name="diff_attn_lat" if latent else "diff_attn_ctx",
    )(*args)


def _time_cumsum(x, reverse):
    L = x.shape[0]
    row = lax.broadcasted_iota(jnp.int32, x.shape, 0)
    sh = 1
    while sh < L:
        if reverse:
            x = x + jnp.where(row < L - sh, pltpu.roll(x, L - sh, 0), 0.0)
        else:
            x = x + jnp.where(row >= sh, pltpu.roll(x, sh, 0), 0.0)
        sh *= 2
    return x


def _mlstm_kernel(*refs, H, dqk, dv, L, reverse, finish, gate_off):
    if finish:
        (q_ref, k_ref, v_ref, gt_ref, bias_ref, hf_ref, og_ref, ng_ref, o_ref, c_ref, n_ref, m_ref) = refs
    else:
        (q_ref, k_ref, v_ref, gt_ref, bias_ref, o_ref, c_ref, n_ref, m_ref) = refs

    @pl.when(pl.program_id(1) == 0)
    def _():
        c_ref[...] = jnp.zeros(c_ref.shape, F32)
        n_ref[...] = jnp.zeros(n_ref.shape, F32)
        m_ref[...] = jnp.zeros(m_ref.shape, F32)

    gates = GATE_CAP * jnp.tanh((gt_ref[...] + bias_ref[...]) * (1.0 / GATE_CAP))
    ig = pltpu.roll(gates, LANES - gate_off, 1) if gate_off else gates
    fg = pltpu.roll(gates, LANES - gate_off - H, 1)
    lf = jnp.minimum(fg, 0.0) - jnp.log1p(jnp.exp(-jnp.abs(fg)))
    bc = _time_cumsum(lf, reverse)
    m_prev = m_ref[0:1, :]
    a_all = bc + m_prev
    b_last = bc[0:1, :] if reverse else bc[L - 1:L, :]
    g_all = b_last - bc + ig
    m_new = jnp.maximum(b_last + m_prev, jnp.max(g_all, axis=0, keepdims=True))
    decay = jnp.exp(b_last + m_prev - m_new)
    eg = jnp.exp(g_all - m_new)
    xt = (ig - bc).T
    m_ref[...] = jnp.broadcast_to(m_new, m_ref.shape)

    li = lax.broadcasted_iota(jnp.int32, (L, L), 0)
    si = lax.broadcasted_iota(jnp.int32, (L, L), 1)
    tri = (si >= li) if reverse else (si <= li)
    qscale = dqk ** -0.5

    for h in range(H):
        q = (q_ref[:, h * dqk:(h + 1) * dqk].astype(F32) * qscale).astype(BF16)
        k = k_ref[:, h * dqk:(h + 1) * dqk]
        v = v_ref[:, h * dv:(h + 1) * dv]
        ct = c_ref[h]
        nv = n_ref[h]
        a_h = a_all[:, h:h + 1]
        dmat = jnp.where(tri, bc[:, h:h + 1] + xt[h:h + 1, :], -jnp.inf)
        m_t = jnp.maximum(a_h, jnp.max(dmat, axis=1, keepdims=True))
        qk = lax.dot_general(q, k, (((1,), (1,)), ((), ())), preferred_element_type=F32)
        w = jnp.exp(dmat - m_t) * qk
        inter = jnp.exp(a_h - m_t)
        num = inter * jnp.dot(q, ct.astype(BF16), preferred_element_type=F32) + jnp.dot(
            w.astype(BF16), v, preferred_element_type=F32)
        qn = jnp.sum(q.astype(F32) * nv, axis=1, keepdims=True)
        den = inter * qn + jnp.sum(w, axis=1, keepdims=True)
        hout = num / jnp.maximum(jnp.abs(den), jnp.exp(-m_t))

        dec_h = decay[:, h:h + 1]
        wk = eg[:, h:h + 1] * k.astype(F32)
        c_ref[h] = dec_h * ct + lax.dot_general(wk.astype(BF16), v, (((0,), (0,)), ((), ())),
                                                preferred_element_type=F32)
        n_ref[h] = dec_h * nv + jnp.sum(wk, axis=0, keepdims=True)

        if finish:
            hs = hout + hf_ref[:, h * dv:(h + 1) * dv].astype(F32)
            hn = _rms(hs, ng_ref[:, h * dv:(h + 1) * dv]).astype(BF16)
            og = og_ref[:, h * dv:(h + 1) * dv]
            o_ref[:, h * dv:(h + 1) * dv] = (hn * jax.nn.sigmoid(og)).astype(o_ref.dtype)
        else:
            o_ref[:, h * dv:(h + 1) * dv] = hout.astype(o_ref.dtype)


def mlstm_scan(cfg, proj, gates, bias, *, reverse, hf=None, norm_g=None):
    D, B, T, Tc, H, L = cfg.D, cfg.B, cfg.T, cfg.Tc, cfg.ml_heads, cfg.chunk
    dqk, dv = cfg.ml_qk, cfg.ml_v
    ncc, ncl = Tc // L, T // L
    nc = ncc + ncl
    finish = hf is not None
    qw = H * dqk
    assert D % qw == 0 and (H * dv) == D

    def rb(b, c):
        if reverse:
            return jnp.where(c < ncc, cfg.n_lat // L + b * ncc + (ncc - 1 - c), b * ncl + (nc - 1 - c))
        return jnp.where(c < ncc, cfg.n_lat // L + b * ncc + c, b * ncl + (c - ncc))

    in_specs = [
        pl.BlockSpec((L, qw), lambda b, c: (rb(b, c), 0)),
        pl.BlockSpec((L, qw), lambda b, c: (rb(b, c), 1)),
        pl.BlockSpec((L, D), lambda b, c: (rb(b, c), 2 * qw // D)),
        pl.BlockSpec((L, LANES), lambda b, c: (rb(b, c), 0)),
        pl.BlockSpec((1, LANES), lambda b, c: (0, 0)),
    ]
    args = [proj, proj, proj, gates, bias]
    if finish:
        in_specs += [
            pl.BlockSpec((L, D), lambda b, c: (rb(b, c), 0)),
            pl.BlockSpec((L, D), lambda b, c: (rb(b, c), 2 * qw // D + 1)),
            pl.BlockSpec((1, D), lambda b, c: (0, 0)),
        ]
        args += [hf, proj, norm_g]
    return pl.pallas_call(
        functools.partial(_mlstm_kernel, H=H, dqk=dqk, dv=dv, L=L, reverse=reverse, finish=finish,
                          gate_off=2 * H if reverse else 0),
        grid=(B, nc),
        in_specs=in_specs,
        out_specs=pl.BlockSpec((L, D), lambda b, c: (rb(b, c), 0)),
        out_shape=jax.ShapeDtypeStruct((cfg.R, D), BF16),
        scratch_shapes=[pltpu.VMEM((H, dqk, dv), F32), pltpu.VMEM((H, 1, dqk), F32), pltpu.VMEM((8, LANES), F32)],
        compiler_params=_cparams(("parallel", "arbitrary")),
        name="mlstm_bwd" if reverse else "mlstm_fwd",
    )(*args)


def _moe_up_kernel(x_ref, wg_ref, wu_ref, o_ref, wgb_ref, wub_ref):
    @pl.when(pl.program_id(2) == 0)
    def _():
        wgb_ref[...] = wg_ref[...].astype(BF16)
        wub_ref[...] = wu_ref[...].astype(BF16)

    x = x_ref[...]
    a = jnp.dot(x, wgb_ref[...], preferred_element_type=F32)
    u = jnp.dot(x, wub_ref[...], preferred_element_type=F32)
    o_ref[...] = (a * jax.nn.sigmoid(a) * u).astype(o_ref.dtype)


def _row_tile(n, cap):
    best = n
    for t in range(16, min(n, cap) + 1, 16):
        if n % t == 0:
            best = t
    return best


def moe_up(cfg, xe, w_gate, w_up, layer):
    E, P, D = xe.shape
    F = cfg.F
    tp = _row_tile(P, 528)
    fk = _tile(F, 512)
    w_spec = pl.BlockSpec((None, None, D, fk), lambda e, f, p: (layer, e, 0, f))
    return pl.pallas_call(
        _moe_up_kernel,
        grid=(E, F // fk, P // tp),
        in_specs=[pl.BlockSpec((None, tp, D), lambda e, f, p: (e, p, 0)), w_spec, w_spec],
        out_specs=pl.BlockSpec((None, tp, fk), lambda e, f, p: (e, p, f)),
        out_shape=jax.ShapeDtypeStruct((E, P, F), BF16),
        scratch_shapes=[pltpu.VMEM((D, fk), BF16), pltpu.VMEM((D, fk), BF16)],
        compiler_params=_cparams(("parallel", "parallel", "arbitrary")),
        name="moe_up",
    )(xe, w_gate, w_up)


def _moe_down_kernel(rows_ref, hid_ref, g_ref, wd_ref, acc_in_ref, acc_ref, wdb_ref, buf_ref, sem_in, sem_out, *,
                     tp, P, nblk):
    del acc_in_ref
    e, p = pl.program_id(0), pl.program_id(1)
    step = e * pl.num_programs(1) + p
    last = pl.num_programs(0) * pl.num_programs(1) - 1
    slot = lax.rem(step, 2)
    base = e * P + p * tp
    blk = tp // nblk

    def tile_copy(s, sem):
        return pltpu.make_async_copy(acc_ref.at[pl.ds(0, tp)], buf_ref.at[s], sem)

    @pl.when(p == 0)
    def _():
        wdb_ref[...] = wd_ref[...].astype(BF16)

        @pl.when(step > 0)
        def _():
            tile_copy(1 - slot, sem_out.at[1 - slot]).wait()

    ys = []
    for i in range(nblk):
        r0 = i * blk
        for r in range(r0, r0 + blk):
            pltpu.make_async_copy(acc_ref.at[pl.ds(rows_ref[base + r], 1)], buf_ref.at[slot, pl.ds(r, 1)],
                                  sem_in).start(priority=r % 2)
        ys.append(jnp.dot(hid_ref[r0:r0 + blk, :], wdb_ref[...], preferred_element_type=F32)
                  * g_ref[r0:r0 + blk, :])
    tile_copy(slot, sem_in).wait()
    for i in range(nblk):
        r0 = i * blk
        buf_ref[slot, r0:r0 + blk, :] += ys[i]
        for r in range(r0, r0 + blk):
            pltpu.make_async_copy(buf_ref.at[slot, pl.ds(r, 1)], acc_ref.at[pl.ds(rows_ref[base + r], 1)],
                                  sem_out.at[slot]).start(priority=r % 2)

    @pl.when(jnp.logical_and(p != 0, step > 0))
    def _():
        tile_copy(1 - slot, sem_out.at[1 - slot]).wait()

    @pl.when(step == last)
    def _():
        tile_copy(slot, sem_out.at[slot]).wait()


def moe_down_scatter(cfg, rows, hid, ge, w_down, layer, n_rows):
    E, P, F = hid.shape
    D = cfg.D
    tp = _row_tile(P, 352)
    grid_spec = pltpu.PrefetchScalarGridSpec(
        num_scalar_prefetch=1,
        grid=(E, P // tp),
        in_specs=[
            pl.BlockSpec((None, tp, F), lambda e, p, rows: (e, p, 0)),
            pl.BlockSpec((None, tp, 1), lambda e, p, rows: (e, p, 0)),
            pl.BlockSpec((None, None, F, D), lambda e, p, rows: (layer, e, 0, 0), pipeline_mode=pl.Buffered(1)),
            pl.BlockSpec(memory_space=pl.ANY),
        ],
        out_specs=pl.BlockSpec(memory_space=pl.ANY),
        scratch_shapes=[pltpu.VMEM((F, D), BF16), pltpu.VMEM((2, tp, D), F32), pltpu.SemaphoreType.DMA(()),
                        pltpu.SemaphoreType.DMA((2,))],
    )
    nblk = 2 if tp % 32 == 0 else 1
    return pl.pallas_call(
        functools.partial(_moe_down_kernel, tp=tp, P=P, nblk=nblk),
        grid_spec=grid_spec,
        out_shape=jax.ShapeDtypeStruct((n_rows, D), F32),
        input_output_aliases={4: 0},
        compiler_params=_cparams(("arbitrary", "arbitrary")),
        name="moe_down_scatter",
    )(rows.reshape(-1), hid, ge, w_down, jnp.zeros((n_rows, D), F32))


def expert_choice_moe(cfg, h, logits, w_gate, w_up, w_down, layer, with_ctx):
    B, T, Tc, E = cfg.B, cfg.T, cfg.Tc, cfg.E
    n_rows = cfg.R if with_ctx else cfg.n_lat
    aff = jax.nn.softmax(logits[:n_rows, :E], axis=-1)
    sets = [(b * T, T) for b in range(B)]
    if with_ctx:
        sets += [(cfg.n_lat + b * Tc, Tc) for b in range(B)]
    rows, gates = [], []
    for start, n in sets:
        cap = max(1, (cfg.cap_factor * n) // E)
        g, idx = lax.top_k(aff[start:start + n].T, cap)
        rows.append(idx + start)
        gates.append(g)
    rows = jnp.concatenate(rows, axis=1)
    gates = jnp.concatenate(gates, axis=1)
    xe = h[rows]
    hid = moe_up(cfg, xe, w_gate, w_up, layer)
    return moe_down_scatter(cfg, rows, hid, gates[..., None], w_down, layer, n_rows)


def _forward(cfg, x, c, ctx, c_ctx, ada_w, ada_b, norm_g, attn_w_in, attn_w_out, attn_lambda, attn_subln_g,
             mlstm_w_in, mlstm_b_gates, mlstm_norm_g, mlstm_w_out, router_w, expert_w_gate, expert_w_up,
             expert_w_down):
    D, B, T, Tc, E = cfg.D, cfg.B, cfg.T, cfg.Tc, cfg.E
    depth = ada_w.shape[0]
    cos_t, sin_t = rope_tables(cfg)
    cvec = jnp.concatenate([c, c_ctx[None, :], jnp.zeros((8 - B - 1, D), F32)], axis=0)
    ada_b3 = ada_b.reshape(depth, 1, 6 * D)
    norm_g3 = norm_g.reshape(depth * 4, 1, D)
    X = (x.reshape(B * T, D), ctx.reshape(B * Tc, D))

    mods = [adaln(cvec, ada_w, ada_b3, i)[:B + 1].reshape((B + 1) * 6, 1, D) for i in range(depth)]
    rw = jnp.pad(router_w, ((0, 0), (0, 0), (0, LANES - E)))
    rw_hi = rw.astype(BF16)
    rw_lo = (rw - rw_hi.astype(F32)).astype(BF16)

    _, h, _ = resid_norm(cfg, X, norm_g3, h_part=(mods[0], 0, 0, 1, 0))
    for i in range(depth):
        last = i == depth - 1
        mod = mods[i]
        j = i // 2
        n_rows = cfg.n_lat if last else cfg.R
        if i % 2 == 0:
            lam_init = 0.8 - 0.6 * math.exp(-0.3 * i)
            w_in = attn_w_in[j]
            w_qk = rope_col_order(cfg, w_in[:, :2 * D]).astype(BF16)
            qk_lat = qk_proj_rope(cfg, h, w_qk, cos_t, sin_t)
            qk_ctx = matmul(h, w_qk, BF16, row0=cfg.n_lat, scale_cols=D,
                            scale=cfg.da_dim ** -0.5 * math.log2(math.e))
            v = matmul_wcast(h, attn_w_in, j, BF16, col0=2 * D, n_cols=D)
            g2 = attn_subln_g[j].reshape(1, -1)
            o = diff_attention(cfg, qk_lat, qk_ctx, v, attn_lambda[j], g2, lam_init, latent=True)
            if not last:
                o = (o, diff_attention(cfg, qk_lat, qk_ctx, v, attn_lambda[j], g2, lam_init, latent=False))
            y = matmul_wcast(o, attn_w_out, j, BF16)
        else:
            H = cfg.ml_heads
            nmain = mlstm_w_in.shape[2] - 4 * H
            proj = matmul_wcast(h, mlstm_w_in, j, BF16, n_cols=nmain)
            wg = jnp.pad(mlstm_w_in[j, :, nmain:], ((0, 0), (0, LANES - 4 * H))).astype(BF16)
            gates = matmul(h, wg, F32)
            bias = jnp.pad(mlstm_b_gates[j].reshape(1, 4 * H), ((0, 0), (0, LANES - 4 * H)))
            hf = mlstm_scan(cfg, proj, gates, bias, reverse=False)
            hn = mlstm_scan(cfg, proj, gates, bias, reverse=True, hf=hf, norm_g=mlstm_norm_g[j].reshape(1, D))
            y = matmul_wcast(hn, mlstm_w_out, j, BF16, n_rows=n_rows)
        X, h, lg = resid_norm(cfg, X, norm_g3, y_part=(y, mod, i, 2, 1), h_part=(mod, i, 2, 4, 3),
                              router=(rw_hi[i], rw_lo[i]), n_rows=n_rows)
        mo = expert_choice_moe(cfg, h, lg, expert_w_gate, expert_w_up, expert_w_down, i, with_ctx=not last)
        if last:
            X, _, _ = resid_norm(cfg, X, norm_g3, y_part=(mo, mod, i, 5, 3), n_rows=n_rows)
        else:
            X, h, _ = resid_norm(cfg, X, norm_g3, y_part=(mo, mod, i, 5, 3), h_part=(mods[i + 1], i + 1, 0, 1, 0))
    return X.reshape(B, T, D)


def kernel(x, c, ctx, c_ctx, ada_w, ada_b, norm_g, attn_w_in, attn_w_out, attn_lambda, attn_subln_g, mlstm_w_in,
           mlstm_b_gates, mlstm_norm_g, mlstm_w_out, router_w, expert_w_gate, expert_w_up, expert_w_down):
    return _forward(FULL_CFG, x, c, ctx, c_ctx, ada_w, ada_b, norm_g, attn_w_in, attn_w_out, attn_lambda,
                    attn_subln_g, mlstm_w_in, mlstm_b_gates, mlstm_norm_g, mlstm_w_out, router_w, expert_w_gate,
                    expert_w_up, expert_w_down)
```
